```python
import math
import jax, jax.numpy as jnp
from jax import lax
import numpy as np

D_MODEL = 1024
BATCH = 8
SEQ = 4096
DEPTH = 4

BRANCH_WIDTH = D_MODEL // 2
N_BRANCH = 3
ATT_HEADS = 4
ATT_HEAD_DIM = BRANCH_WIDTH // (2 * ATT_HEADS)
ATT_V_DIM = 2 * ATT_HEAD_DIM
ROPE_THETA = 10000.0
Q_BLOCK = 128
HY_WIDTH = BRANCH_WIDTH
HY_ORDER = 2
HY_SHORT = 3
HY_BANDS = 16
HY_EMB = 1 + 2 * HY_BANDS
HY_FILTER_HIDDEN = 64
HY_FAST_DECAY = 0.3
HY_SLOW_DECAY = 1.5
HY_TARGET = 1e-2
HY_SHIFT = 0.05
CF_WIDTH = BRANCH_WIDTH
CF_KERNEL = 31
Q_COLS = ATT_HEADS * 2 * ATT_HEAD_DIM
K_COLS = ATT_HEADS * 2 * ATT_HEAD_DIM
V_COLS = ATT_HEADS * ATT_V_DIM
HY_COLS = (HY_ORDER + 1) * HY_WIDTH
CF_COLS = 2 * CF_WIDTH
GATE_COLS = N_BRANCH * D_MODEL
IN_COLS = Q_COLS + K_COLS + V_COLS + HY_COLS + CF_COLS + GATE_COLS
N_EXPERTS = 32
TOP_K = 4
EXPERT_FF = D_MODEL
SWIGLU_LIMIT = 7.0
SWIGLU_ALPHA = 1.702
EPS = 1e-6

kernel_name = 'hybrid_parallel_diffattn_hyena_conformer_moe'


def rmsnorm(x, g):
    xf = x.astype(jnp.float32)
    y = xf * lax.rsqrt(jnp.mean(xf * xf, axis=-1, keepdims=True) + EPS)
    return (y * g.astype(jnp.float32)).astype(x.dtype)


def layernorm(x, g, b):
    xf = x.astype(jnp.float32)
    mu = jnp.mean(xf, axis=-1, keepdims=True)
    var = jnp.mean(jnp.square(xf - mu), axis=-1, keepdims=True)
    y = (xf - mu) * lax.rsqrt(var + EPS)
    return (y * g.astype(jnp.float32) + b.astype(jnp.float32)).astype(x.dtype)


def dwconv(u, w, b):
    k, ch = w.shape
    y = lax.conv_general_dilated(u, w[:, None, :], window_strides=(1,),
                                 padding=[(k // 2, k // 2)],
                                 dimension_numbers=('NWC', 'WIO', 'NWC'),
                                 feature_group_count=ch)
    return y + b


def rope_tables(seq, dtype):
    pos = jnp.arange(seq, dtype=jnp.float32)
    inv = 1.0 / (ROPE_THETA ** (jnp.arange(0, ATT_HEAD_DIM, 2, dtype=jnp.float32) / ATT_HEAD_DIM))
    ang = pos[:, None] * inv[None, :]
    ang = jnp.concatenate([ang, ang], axis=-1)
    return jnp.cos(ang).astype(dtype), jnp.sin(ang).astype(dtype)


def apply_rope(t, cos, sin):
    c = cos[None, :, None, None, :]
    s = sin[None, :, None, None, :]
    t1, t2 = jnp.split(t, 2, axis=-1)
    return t * c + jnp.concatenate([-t2, t1], axis=-1) * s


def diff_attention(q, k, v, lam, lam_init, subln_g):
    bsz, seq, heads, _, d = q.shape
    n_blk = seq // Q_BLOCK
    qb = q.reshape(bsz, n_blk, Q_BLOCK, heads, 2, d).transpose(1, 0, 2, 3, 4, 5)
    scale = d ** -0.5

    def block(qi):
        s = jnp.einsum('bqhcd,bkhcd->bhcqk', qi, k).astype(jnp.float32) * scale
        p = jax.nn.softmax(s, axis=-1)
        a = (p[:, :, 0] - lam * p[:, :, 1]).astype(v.dtype)
        return jnp.einsum('bhqk,bkhe->bqhe', a, v)

    o = lax.map(block, qb)
    o = o.transpose(1, 0, 2, 3, 4).reshape(bsz, seq, heads, ATT_V_DIM)
    o = rmsnorm(o, subln_g) * (1.0 - lam_init)
    return o.reshape(bsz, seq, heads * ATT_V_DIM)


def hyena_filters(seq, w1, b1, w2, b2, w3, b3, freq, w4):
    f32 = jnp.float32
    pos = jnp.arange(seq, dtype=f32)
    t = jnp.linspace(0.0, 1.0, seq, dtype=f32)[:, None]
    bands = jnp.linspace(1e-4, HY_BANDS - 1, HY_BANDS, dtype=f32)
    ang = (2.0 * math.pi / seq) * pos[:, None] * bands[None, :]
    z = jnp.concatenate([t, jnp.cos(ang), -jnp.sin(ang)], axis=-1).astype(w1.dtype)
    hdn = jnp.sin(freq * (z @ w1 + b1))
    hdn = jnp.sin(freq * (hdn @ w2 + b2))
    hdn = jnp.sin(freq * (hdn @ w3 + b3))
    h = (hdn @ w4).reshape(seq, HY_ORDER, 2, HY_WIDTH)
    deltas = jnp.abs(jnp.linspace(math.log(HY_TARGET) / HY_FAST_DECAY,
                                  math.log(HY_TARGET) / HY_SLOW_DECAY, HY_WIDTH, dtype=f32))
    window = (jnp.exp(-t * deltas[None, :]) + HY_SHIFT).astype(h.dtype)
    return h * window[:, None, None, :]


def bidir_long_conv(u, h_fwd, h_bwd):
    seq = u.shape[1]
    n = 2 * seq
    kern = jnp.concatenate([h_fwd[:1] + h_bwd[:1], h_fwd[1:], jnp.zeros_like(h_fwd[:1]),
                            h_bwd[:0:-1]], axis=0).astype(jnp.float32)
    uf = jnp.fft.rfft(u.astype(jnp.float32), n=n, axis=1)
    kf = jnp.fft.rfft(kern, n=n, axis=0)
    y = jnp.fft.irfft(uf * kf[None], n=n, axis=1)[:, :seq]
    return y.astype(u.dtype)


def hyena_mixer(u, conv_w, conv_b, filters, d_bias):
    u = dwconv(u, conv_w, conv_b)
    v, x1, x2 = jnp.split(u, 3, axis=-1)
    z = v
    for n, xg in enumerate((x1, x2)):
        z = xg * (bidir_long_conv(z, filters[:, n, 0], filters[:, n, 1]) + d_bias[n] * z)
    return z


def conformer_conv(u, dw_w, dw_b, ln_g, ln_b):
    a, g = jnp.split(u, 2, axis=-1)
    u = a * jax.nn.sigmoid(g)
    u = dwconv(u, dw_w, dw_b)
    u = layernorm(u, ln_g, ln_b)
    return jax.nn.silu(u)


def moe(h, w_r, b_r, w_gu, b_gu, w_dn, b_dn):
    bsz, seq, d = h.shape
    t = h.reshape(bsz * seq, d)
    logits = (t @ w_r + b_r).astype(jnp.float32)
    top_v, top_i = lax.top_k(logits, TOP_K)
    top_w = jax.nn.softmax(top_v, axis=-1)
    comb = jnp.sum(jax.nn.one_hot(top_i, N_EXPERTS, dtype=jnp.float32) * top_w[..., None],
                   axis=1).astype(t.dtype)

    def body(e, acc):
        gu = t @ w_gu[e] + b_gu[e]
        g, u = jnp.split(gu, 2, axis=-1)
        g = jnp.minimum(g, SWIGLU_LIMIT)
        u = jnp.clip(u, -SWIGLU_LIMIT, SWIGLU_LIMIT)
        hdn = (u + 1.0) * (g * jax.nn.sigmoid(SWIGLU_ALPHA * g))
        y = hdn @ w_dn[e] + b_dn[e]
        return acc + comb[:, e][:, None] * y

    out = lax.fori_loop(0, N_EXPERTS, body, jnp.zeros_like(t))
    return out.reshape(bsz, seq, d)


def setup_inputs(seed: int = 0) -> dict:
    key = jax.random.key(seed)
    ks = iter(jax.random.split(key, 48))
    f32 = jnp.float32
    D = D_MODEL
    L = DEPTH

    def nrm(shape, scale):
        return jax.random.normal(next(ks), shape, f32) * scale

    def gain(shape):
        return 1.0 + nrm(shape, 0.02)

    return {
        'x': nrm((BATCH, SEQ, D), 1.0),
        'c': nrm((BATCH, D), 1.0),
        'norm1_g': gain((L, D)),
        'norm2_g': gain((L, D)),
        'w_ada': nrm((L, D, 6 * D), 0.5 * D ** -0.5),
        'b_ada': nrm((L, 6 * D), 0.01),
        'w_in': nrm((L, D, IN_COLS), D ** -0.5),
        'b_in': nrm((L, IN_COLS), 0.02),
        'q_norm_g': gain((L, ATT_HEAD_DIM)),
        'k_norm_g': gain((L, ATT_HEAD_DIM)),
        'lam_q1': nrm((L, ATT_HEAD_DIM), 0.1),
        'lam_k1': nrm((L, ATT_HEAD_DIM), 0.1),
        'lam_q2': nrm((L, ATT_HEAD_DIM), 0.1),
        'lam_k2': nrm((L, ATT_HEAD_DIM), 0.1),
        'subln_g': gain((L, ATT_V_DIM)),
        'hy_conv_w': nrm((L, HY_SHORT, HY_COLS), HY_SHORT ** -0.5),
        'hy_conv_b': nrm((L, HY_COLS), 0.02),
        'hf_w1': nrm((L, HY_EMB, HY_FILTER_HIDDEN), HY_EMB ** -0.5),
        'hf_b1': nrm((L, HY_FILTER_HIDDEN), 0.02),
        'hf_w2': nrm((L, HY_FILTER_HIDDEN, HY_FILTER_HIDDEN), HY_FILTER_HIDDEN ** -0.5),
        'hf_b2': nrm((L, HY_FILTER_HIDDEN), 0.02),
        'hf_w3': nrm((L, HY_FILTER_HIDDEN, HY_FILTER_HIDDEN), HY_FILTER_HIDDEN ** -0.5),
        'hf_b3': nrm((L, HY_FILTER_HIDDEN), 0.02),
        'hf_freq': gain((L, HY_FILTER_HIDDEN)),
        'hf_w4': nrm((L, HY_FILTER_HIDDEN, HY_ORDER * 2 * HY_WIDTH), 0.05 * HY_FILTER_HIDDEN ** -0.5),
        'hy_bias': nrm((L, HY_ORDER, HY_WIDTH), 0.5),
        'cf_dw_w': nrm((L, CF_KERNEL, CF_WIDTH), CF_KERNEL ** -0.5),
        'cf_dw_b': nrm((L, CF_WIDTH), 0.02),
        'cf_ln_g': gain((L, CF_WIDTH)),
        'cf_ln_b': nrm((L, CF_WIDTH), 0.02),
        'w_branch': nrm((L, N_BRANCH, BRANCH_WIDTH, D), BRANCH_WIDTH ** -0.5),
        'w_out': nrm((L, D, D), D ** -0.5),
        'w_router': nrm((L, D, N_EXPERTS), D ** -0.5),
        'b_router': nrm((L, N_EXPERTS), 0.01),
        'w_gu': nrm((L, N_EXPERTS, D, 2 * EXPERT_FF), D ** -0.5),
        'b_gu': nrm((L, N_EXPERTS, 2 * EXPERT_FF), 0.02),
        'w_down': nrm((L, N_EXPERTS, EXPERT_FF, D), EXPERT_FF ** -0.5),
        'b_down': nrm((L, N_EXPERTS, D), 0.02),
    }


def reference(x, c, norm1_g, norm2_g, w_ada, b_ada, w_in, b_in, q_norm_g, k_norm_g,
              lam_q1, lam_k1, lam_q2, lam_k2, subln_g, hy_conv_w, hy_conv_b,
              hf_w1, hf_b1, hf_w2, hf_b2, hf_w3, hf_b3, hf_freq, hf_w4, hy_bias,
              cf_dw_w, cf_dw_b, cf_ln_g, cf_ln_b, w_branch, w_out,
              w_router, b_router, w_gu, b_gu, w_down, b_down):
    bsz, seq, _ = x.shape
    cos, sin = rope_tables(seq, x.dtype)
    cond = jax.nn.silu(c)
    splits = np.cumsum([Q_COLS, K_COLS, V_COLS, HY_COLS, CF_COLS]).tolist()
    for l in range(DEPTH):
        mod = cond @ w_ada[l] + b_ada[l]
        sh1, sc1, g1, sh2, sc2, g2 = jnp.split(mod[:, None, :], 6, axis=-1)

        h = rmsnorm(x, norm1_g[l]) * (1.0 + sc1) + sh1
        proj = h @ w_in[l] + b_in[l]
        q, k, v, hy_in, cf_in, gates = jnp.split(proj, splits, axis=-1)

        q = q.reshape(bsz, seq, ATT_HEADS, 2, ATT_HEAD_DIM)
        k = k.reshape(bsz, seq, ATT_HEADS, 2, ATT_HEAD_DIM)
        v = v.reshape(bsz, seq, ATT_HEADS, ATT_V_DIM)
        q = apply_rope(rmsnorm(q, q_norm_g[l]), cos, sin)
        k = apply_rope(rmsnorm(k, k_norm_g[l]), cos, sin)
        lam_init = 0.8 - 0.6 * math.exp(-0.3 * l)
        lam = (jnp.exp(jnp.sum((lam_q1[l] * lam_k1[l]).astype(jnp.float32)))
               - jnp.exp(jnp.sum((lam_q2[l] * lam_k2[l]).astype(jnp.float32))) + lam_init)
        o_att = diff_attention(q, k, v, lam, lam_init, subln_g[l])

        filt = hyena_filters(seq, hf_w1[l], hf_b1[l], hf_w2[l], hf_b2[l], hf_w3[l], hf_b3[l],
                             hf_freq[l], hf_w4[l])
        o_hy = hyena_mixer(hy_in, hy_conv_w[l], hy_conv_b[l], filt, hy_bias[l])

        o_cf = conformer_conv(cf_in, cf_dw_w[l], cf_dw_b[l], cf_ln_g[l], cf_ln_b[l])

        br = jnp.stack([o_att, o_hy, o_cf], axis=2)
        gate = jax.nn.sigmoid(gates.reshape(bsz, seq, N_BRANCH, D_MODEL))
        merged = jnp.sum(jnp.einsum('bsnw,nwd->bsnd', br, w_branch[l]) * gate, axis=2)
        x = x + g1 * (merged @ w_out[l])

        h = rmsnorm(x, norm2_g[l]) * (1.0 + sc2) + sh2
        x = x + g2 * moe(h, w_router[l], b_router[l], w_gu[l], b_gu[l], w_down[l], b_down[l])
    return x
```

```python
import functools
import math

import jax
import jax.numpy as jnp
from jax import lax
from jax.experimental import pallas as pl
from jax.experimental.pallas import tpu as pltpu

F32 = jnp.float32
BF16 = jnp.bfloat16
I32 = jnp.int32

LANES = 128
VMEM_LIMIT_BYTES = 56 * 1024 * 1024

ATT_HEADS = 4
HEAD_DIM = 64
V_DIM = 2 * HEAD_DIM
ROPE_THETA = 10000.0
HY_BANDS = 16
HY_FAST_DECAY = 0.3
HY_SLOW_DECAY = 1.5
HY_TARGET = 1e-2
HY_SHIFT = 0.05
CF_KERNEL = 31
CF_HALO = 16
TOP_K = 4
SWIGLU_LIMIT = 7.0
SWIGLU_ALPHA = 1.702
EPS = 1e-6
LOG2E = 1.4426950408889634


def _cparams(*sem):
    return pltpu.CompilerParams(dimension_semantics=sem, vmem_limit_bytes=VMEM_LIMIT_BYTES)


def _tile(n, want):
    t = min(n, want)
    while n % t:
        t -= 1
    return t


def _ada_kernel(c_ref, w_ref, b_ref, o_ref):
    c = c_ref[...]
    cond = c * jax.nn.sigmoid(c)
    o_ref[0] = jnp.dot(cond.astype(BF16), w_ref[0].astype(BF16),
                       preferred_element_type=F32) + b_ref[0]


def _ada_all(c, w_ada, b_ada):
    depth, d, n = w_ada.shape
    bsz = c.shape[0]
    tn = _tile(n, 1536)
    return pl.pallas_call(
        _ada_kernel,
        grid=(depth, n // tn),
        in_specs=[pl.BlockSpec((bsz, d), lambda l, j: (0, 0)),
                  pl.BlockSpec((1, d, tn), lambda l, j: (l, 0, j)),
                  pl.BlockSpec((1, 1, tn), lambda l, j: (l, 0, j))],
        out_specs=pl.BlockSpec((1, bsz, tn), lambda l, j: (l, 0, j)),
        out_shape=jax.ShapeDtypeStruct((depth, bsz, n), F32),
        compiler_params=_cparams("arbitrary", "arbitrary"),
        name="ada_mod",
    )(c, w_ada, b_ada.reshape(depth, 1, n))


def _proj_kernel(x_ref, sh_ref, sc_ref, g_ref, w_ref, b_ref, o_ref, h_scr):
    @pl.when(pl.program_id(1) == 0)
    def _():
        x = x_ref[...]
        y = x * lax.rsqrt(jnp.mean(x * x, axis=-1, keepdims=True) + EPS)
        h = (y * g_ref[...]) * (1.0 + sc_ref[0]) + sh_ref[0]
        h_scr[...] = h.astype(BF16)

    acc = jnp.dot(h_scr[...], w_ref[...], preferred_element_type=F32)
    o_ref[...] = (acc + b_ref[...]).astype(o_ref.dtype)


def _in_proj(x2, mod3, norm_g, w_bf, b, seq):
    t, d = x2.shape
    n = w_bf.shape[1]
    tm = _tile(seq, 1024)
    tn = _tile(n, 1792)
    per_b = seq // tm
    return pl.pallas_call(
        _proj_kernel,
        grid=(t // tm, n // tn),
        in_specs=[pl.BlockSpec((tm, d), lambda i, j: (i, 0)),
                  pl.BlockSpec((1, 1, d), lambda i, j: (i // per_b, 0, 0)),
                  pl.BlockSpec((1, 1, d), lambda i, j: (i // per_b, 0, 1)),
                  pl.BlockSpec((1, d), lambda i, j: (0, 0)),
                  pl.BlockSpec((d, tn), lambda i, j: (0, j)),
                  pl.BlockSpec((1, tn), lambda i, j: (0, j))],
        out_specs=pl.BlockSpec((tm, tn), lambda i, j: (i, j)),
        out_shape=jax.ShapeDtypeStruct((t, n), BF16),
        scratch_shapes=[pltpu.VMEM((tm, d), BF16)],
        compiler_params=_cparams("arbitrary", "arbitrary"),
        name="in_proj",
    )(x2, mod3, mod3, norm_g.reshape(1, d), w_bf, b.reshape(1, n))


def _qkprep_kernel(q_ref, k_ref, cos_ref, sin_ref, gq_ref, gk_ref, gs_ref, qo_ref, kto_ref, *, q_scale):
    width = q_ref.shape[-1]
    lane = lax.broadcasted_iota(I32, q_ref.shape[1:], 1)
    first_half = (lane % HEAD_DIM) < (HEAD_DIM // 2)

    def prep(t, g):
        ssq = jnp.dot((t * t).astype(BF16), gs_ref[...], preferred_element_type=F32)
        y = t * lax.rsqrt(ssq * (1.0 / HEAD_DIM) + EPS) * g
        rot = jnp.where(first_half,
                        pltpu.roll(y, width - HEAD_DIM // 2, 1),
                        pltpu.roll(y, HEAD_DIM // 2, 1))
        return y * cos_ref[...] + rot * sin_ref[...]

    q = prep(q_ref[0].astype(F32), gq_ref[...])
    qo_ref[0] = (q * q_scale).astype(BF16)
    k = prep(k_ref[0].astype(F32), gk_ref[...])
    kto_ref[0] = k.T.astype(BF16)


def _qk_prep(proj3, cos_t, sin_t, gq, gk, gsum):
    bsz, seq, _ = proj3.shape
    width = ATT_HEADS * 2 * HEAD_DIM
    tr = _tile(seq, 512)
    q_scale = HEAD_DIM ** -0.5 * LOG2E
    return pl.pallas_call(
        functools.partial(_qkprep_kernel, q_scale=q_scale),
        grid=(bsz, seq // tr),
        in_specs=[pl.BlockSpec((1, tr, width), lambda b, i: (b, i, 0)),
                  pl.BlockSpec((1, tr, width), lambda b, i: (b, i, 1)),
                  pl.BlockSpec((tr, width), lambda b, i: (i, 0)),
                  pl.BlockSpec((tr, width), lambda b, i: (i, 0)),
                  pl.BlockSpec((1, width), lambda b, i: (0, 0)),
                  pl.BlockSpec((1, width), lambda b, i: (0, 0)),
                  pl.BlockSpec((width, width), lambda b, i: (0, 0))],
        out_specs=[pl.BlockSpec((1, tr, width), lambda b, i: (b, i, 0)),
                   pl.BlockSpec((1, width, tr), lambda b, i: (b, 0, i))],
        out_shape=[jax.ShapeDtypeStruct((bsz, seq, width), BF16),
                   jax.ShapeDtypeStruct((bsz, width, seq), BF16)],
        compiler_params=_cparams("arbitrary", "arbitrary"),
        name="qk_prep",
    )(proj3, proj3, cos_t, sin_t, gq, gk, gsum)


def _attn_kernel(q_ref, kt_ref, v_ref, lamv_ref, g_ref, o_ref, *, lam_init):
    lamv = lamv_ref[...]
    lam = (jnp.exp(jnp.sum(lamv[0:1] * lamv[1:2], axis=-1, keepdims=True))
           - jnp.exp(jnp.sum(lamv[2:3] * lamv[3:4], axis=-1, keepdims=True)) + lam_init)
    q = q_ref[0]
    for h in range(ATT_HEADS):
        probs = []
        for comp in range(2):
            lo = (2 * h + comp) * HEAD_DIM
            s = jnp.dot(q[:, lo:lo + HEAD_DIM], kt_ref[0, lo:lo + HEAD_DIM, :],
                        preferred_element_type=F32)
            p = jnp.exp2(s - jnp.max(s, axis=-1, keepdims=True))
            probs.append((p, jnp.sum(p, axis=-1, keepdims=True)))
        (p1, l1), (p2, l2) = probs
        a = p1 * (1.0 / l1) - p2 * (lam / l2)
        o = jnp.dot(a.astype(BF16), v_ref[0, :, h * V_DIM:(h + 1) * V_DIM],
                    preferred_element_type=F32)
        o = o * lax.rsqrt(jnp.mean(o * o, axis=-1, keepdims=True) + EPS)
        o_ref[0, :, h * V_DIM:(h + 1) * V_DIM] = (o * g_ref[...] * (1.0 - lam_init)).astype(o_ref.dtype)


def _attention(qr, kt, proj3, lamv, subln_g, lam_init):
    bsz, seq, width = qr.shape
    tq = _tile(seq, 256)
    return pl.pallas_call(
        functools.partial(_attn_kernel, lam_init=lam_init),
        grid=(bsz, seq // tq),
        in_specs=[pl.BlockSpec((1, tq, width), lambda b, i: (b, i, 0)),
                  pl.BlockSpec((1, width, seq), lambda b, i: (b, 0, 0)),
                  pl.BlockSpec((1, seq, width), lambda b, i: (b, 0, 2)),
                  pl.BlockSpec((4, HEAD_DIM), lambda b, i: (0, 0)),
                  pl.BlockSpec((1, V_DIM), lambda b, i: (0, 0))],
        out_specs=pl.BlockSpec((1, tq, width), lambda b, i: (b, i, 0)),
        out_shape=jax.ShapeDtypeStruct((bsz, seq, width), BF16),
        compiler_params=_cparams("arbitrary", "arbitrary"),
        name="diff_attn",
    )(qr, kt, proj3, lamv, subln_g.reshape(1, V_DIM))


def _hyshort_kernel(u_ref, w_ref, b_ref, o_ref):
    u = u_ref[0].astype(F32)
    seq = u.shape[0]
    row = lax.broadcasted_iota(I32, u.shape, 0)
    prev = jnp.where(row == 0, 0.0, pltpu.roll(u, 1, 0))
    nxt = jnp.where(row == seq - 1, 0.0, pltpu.roll(u, seq - 1, 0))
    w = w_ref[...]
    y = w[0:1] * prev + w[1:2] * u + w[2:3] * nxt + b_ref[...]
    o_ref[0, 0] = y.astype(o_ref.dtype)


def _hy_short(proj3, conv_w, conv_b, col0, width):
    bsz, seq, _ = proj3.shape
    cb = LANES
    per = width // cb
    return pl.pallas_call(
        _hyshort_kernel,
        grid=(bsz, 3 * per),
        in_specs=[pl.BlockSpec((1, seq, cb), lambda b, j: (b, 0, col0 // cb + j)),
                  pl.BlockSpec((3, cb), lambda b, j: (0, j)),
                  pl.BlockSpec((1, cb), lambda b, j: (0, j))],
        out_specs=pl.BlockSpec((1, 1, seq, cb), lambda b, j: (j // per, b, 0, j % per)),
        out_shape=jax.ShapeDtypeStruct((3, bsz, seq, width), BF16),
        compiler_params=_cparams("arbitrary", "arbitrary"),
        name="hy_short",
    )(proj3, conv_w, conv_b.reshape(1, 3 * width))


def _filt_kernel(zz_ref, w1_ref, b1_ref, w2_ref, b2_ref, w3_ref, b3_ref, fr_ref,
                 w4f_ref, w4b_ref, wf_ref, wb_ref, o_ref):
    hi = lax.Precision.HIGHEST
    fr = fr_ref[...]
    h = jnp.sin(fr * (jnp.dot(zz_ref[...], w1_ref[...], precision=hi, preferred_element_type=F32) + b1_ref[...]))
    h = jnp.sin(fr * (jnp.dot(h, w2_ref[...], precision=hi, preferred_element_type=F32) + b2_ref[...]))
    h = jnp.sin(fr * (jnp.dot(h, w3_ref[...], precision=hi, preferred_element_type=F32) + b3_ref[...]))
    ff = jnp.dot(h, w4f_ref[...], precision=hi, preferred_element_type=F32)
    fb = jnp.dot(h, w4b_ref[...], precision=hi, preferred_element_type=F32)
    width = wf_ref.shape[-1]
    wf = wf_ref[...]
    wb = wb_ref[...]
    for n in range(2):
        sl = slice(n * width, (n + 1) * width)
        o_ref[0, n] = (ff[:, sl] * wf + fb[:, sl] * wb).astype(o_ref.dtype)


def _hy_kernels(tabs, w1p, b1, w2, b2, w3, b3, freq, w4f, w4b, seq, width):
    zz, wf, wb = tabs
    hid = w2.shape[0]
    tr = _tile(seq, 512)
    nt = seq // tr
    emb = zz.shape[1]
    full = lambda shape: pl.BlockSpec(shape, lambda hf, i: tuple(0 for _ in shape))
    return pl.pallas_call(
        _filt_kernel,
        grid=(2, nt),
        in_specs=[pl.BlockSpec((tr, emb), lambda hf, i: (hf * nt + i, 0)),
                  full((emb, hid)), full((1, hid)), full((hid, hid)), full((1, hid)),
                  full((hid, hid)), full((1, hid)), full((1, hid)),
                  full((hid, 2 * width)), full((hid, 2 * width)),
                  pl.BlockSpec((tr, width), lambda hf, i: (hf * nt + i, 0)),
                  pl.BlockSpec((tr, width), lambda hf, i: (hf * nt + i, 0))],
        out_specs=pl.BlockSpec((1, 2, tr, width), lambda hf, i: (hf, 0, i, 0)),
        out_shape=jax.ShapeDtypeStruct((2, 2, seq, width), BF16),
        compiler_params=_cparams("arbitrary", "arbitrary"),
        name="hy_filter",
    )(zz, w1p, b1.reshape(1, hid), w2, b2.reshape(1, hid), w3, b3.reshape(1, hid),
      freq.reshape(1, hid), w4f, w4b, wf, wb)


def _fft_a_kernel(m_ref, xa_ref, xb_ref, o_ref):
    d = jnp.concatenate([xa_ref[0].astype(BF16), xb_ref[0].astype(BF16)], axis=0)
    o_ref[0] = jnp.dot(m_ref[...], d, preferred_element_type=F32).astype(o_ref.dtype)


def _fft_a(m1, xa, xb, pairs, amap, bmap):
    nh, cols = xa.shape[1], xa.shape[2]
    rows = m1.shape[0]
    tc = _tile(cols, 8192)
    return pl.pallas_call(
        _fft_a_kernel,
        grid=(cols // tc, pairs),
        in_specs=[pl.BlockSpec(m1.shape, lambda j, p: (0, 0)),
                  pl.BlockSpec((1, nh, tc), lambda j, p: (amap(p), 0, j)),
                  pl.BlockSpec((1, nh, tc), lambda j, p: (bmap(p), 0, j))],
        out_specs=pl.BlockSpec((1, rows, tc), lambda j, p: (p, 0, j)),
        out_shape=jax.ShapeDtypeStruct((pairs, rows, cols), BF16),
        compiler_params=_cparams("arbitrary", "arbitrary"),
        name="fft_a",
    )(m1, xa, xb)


def _fft_b_kernel(are_ref, aim_ref, g_ref, gi_ref, kf_ref, vre_ref, vim_ref):
    kb = are_ref.shape[1]
    half = are_ref.shape[2]
    for q in range(kb):
        d = jnp.concatenate([are_ref[0, q], aim_ref[0, q]], axis=0)
        x = jnp.dot(g_ref[q], d, preferred_element_type=F32)
        xr, xi = x[:half], x[half:]
        kr, ki = kf_ref[q, :half], kf_ref[q, half:]
        p = jnp.concatenate([xr * kr - xi * ki, xr * ki + xi * kr], axis=0).astype(BF16)
        v = jnp.dot(gi_ref[q], p, preferred_element_type=F32)
        vre_ref[0, q] = v[:half].astype(vre_ref.dtype)
        vim_ref[0, q] = v[half:].astype(vim_ref.dtype)


def _fft_b(a4, g, gi, kf, n2):
    pairs, _, half, width = a4.shape
    kb = _tile(n2, 8)
    nk = n2 // kb
    return pl.pallas_call(
        _fft_b_kernel,
        grid=(nk, pairs),
        in_specs=[pl.BlockSpec((1, kb, half, width), lambda i, p: (p, i, 0, 0)),
                  pl.BlockSpec((1, kb, half, width), lambda i, p: (p, nk + i, 0, 0)),
                  pl.BlockSpec((kb, 2 * half, 2 * half), lambda i, p: (i, 0, 0)),
                  pl.BlockSpec((kb, 2 * half, 2 * half), lambda i, p: (i, 0, 0)),
                  pl.BlockSpec((kb, 2 * half, width), lambda i, p: (i, 0, 0))],
        out_specs=[pl.BlockSpec((1, kb, half, width), lambda i, p: (p, i, 0, 0)),
                   pl.BlockSpec((1, kb, half, width), lambda i, p: (p, i, 0, 0))],
        out_shape=[jax.ShapeDtypeStruct((pairs, n2, half, width), BF16),
                   jax.ShapeDtypeStruct((pairs, n2, half, width), BF16)],
        compiler_params=_cparams("arbitrary", "arbitrary"),
        name="fft_b",
    )(a4, a4, g, gi, kf)


def _fft_bf_kernel(are_ref, aim_ref, g_ref, kf_ref):
    kb = are_ref.shape[1]
    for q in range(kb):
        d = jnp.concatenate([are_ref[0, q], aim_ref[0, q]], axis=0)
        kf_ref[0, q] = jnp.dot(g_ref[q], d, preferred_element_type=F32)


def _fft_bf(a4, g, n2):
    pairs, _, half, width = a4.shape
    kb = _tile(n2, 8)
    nk = n2 // kb
    return pl.pallas_call(
        _fft_bf_kernel,
        grid=(nk, pairs),
        in_specs=[pl.BlockSpec((1, kb, half, width), lambda i, p: (p, i, 0, 0)),
                  pl.BlockSpec((1, kb, half, width), lambda i, p: (p, nk + i, 0, 0)),
                  pl.BlockSpec((kb, 2 * half, 2 * half), lambda i, p: (i, 0, 0))],
        out_specs=pl.BlockSpec((1, kb, 2 * half, width), lambda i, p: (p, i, 0, 0)),
        out_shape=jax.ShapeDtypeStruct((pairs, n2, 2 * half, width), F32),
        compiler_params=_cparams("arbitrary", "arbitrary"),
        name="fft_bf",
    )(a4, a4, g)


def _fft_c_kernel(m_ref, vre_ref, vim_ref, xg_ref, z_ref, db_ref, o_ref):
    d = jnp.concatenate([vre_ref[0], vim_ref[0]], axis=0)
    y = jnp.dot(m_ref[...], d, preferred_element_type=F32)
    nh = y.shape[0] // 2
    for s in range(2):
        conv = y[s * nh:(s + 1) * nh]
        z = z_ref[s].astype(F32)
        o_ref[s] = (xg_ref[s].astype(F32) * (conv + db_ref[...] * z)).astype(o_ref.dtype)


def _fft_c(m2, vre, vim, xg, z, dbias_t):
    bsz, nh, cols = z.shape
    pairs, n2, _ = vre.shape
    tc = _tile(cols, 8192)
    return pl.pallas_call(
        _fft_c_kernel,
        grid=(cols // tc, pairs),
        in_specs=[pl.BlockSpec(m2.shape, lambda j, p: (0, 0)),
                  pl.BlockSpec((1, n2, tc), lambda j, p: (p, 0, j)),
                  pl.BlockSpec((1, n2, tc), lambda j, p: (p, 0, j)),
                  pl.BlockSpec((2, nh, tc), lambda j, p: (p, 0, j)),
                  pl.BlockSpec((2, nh, tc), lambda j, p: (p, 0, j)),
                  pl.BlockSpec((1, tc), lambda j, p: (0, j))],
        out_specs=pl.BlockSpec((2, nh, tc), lambda j, p: (p, 0, j)),
        out_shape=jax.ShapeDtypeStruct((bsz, nh, cols), BF16),
        compiler_params=_cparams("arbitrary", "arbitrary"),
        name="fft_c",
    )(m2, vre, vim, xg, z, dbias_t)


def _stack_complex(zr, zi):
    return jnp.concatenate([jnp.concatenate([zr, -zi], axis=-1),
                            jnp.concatenate([zi, zr], axis=-1)], axis=-2)


def _dft_tables(seq):
    n = 2 * seq
    n2 = n // LANES
    nh = n2 // 2
    ar = lambda m: jnp.arange(m, dtype=I32)

    def cis(num, den, sign):
        ang = (2.0 * math.pi / den) * (num % den).astype(F32)
        return jnp.cos(ang), sign * jnp.sin(ang)

    fr, fi = cis(ar(n2)[:, None] * ar(n2)[None, :], n2, -1.0)
    m1_data = _stack_complex(fr[:, :nh], fi[:, :nh])
    m1_filt = jnp.concatenate([fr, fi], axis=0)
    m2 = _stack_complex(fr.T[:nh] / n, -fi.T[:nh] / n)
    k2 = ar(n2)[:, None, None]
    k1 = ar(LANES)[None, :, None]
    n1 = ar(LANES)[None, None, :]
    er, ei = cis(n1 * (n2 * k1 + k2), n, -1.0)
    g = _stack_complex(er, ei)
    gi = _stack_complex(jnp.swapaxes(er, 1, 2), -jnp.swapaxes(ei, 1, 2))
    return (m1_data.astype(BF16), m1_filt.astype(BF16), m2.astype(BF16), g.astype(BF16), gi.astype(BF16), n2)


def _hy_tables(seq, width):
    pos = jnp.arange(seq, dtype=F32)
    t = jnp.linspace(0.0, 1.0, seq, dtype=F32)[:, None]
    bands = jnp.linspace(1e-4, HY_BANDS - 1, HY_BANDS, dtype=F32)
    ang = (2.0 * math.pi / seq) * pos[:, None] * bands[None, :]
    z = jnp.concatenate([t, jnp.cos(ang), -jnp.sin(ang)], axis=-1)
    deltas = jnp.abs(jnp.linspace(math.log(HY_TARGET) / HY_FAST_DECAY,
                                  math.log(HY_TARGET) / HY_SLOW_DECAY, width, dtype=F32))
    window = jnp.exp(-t * deltas[None, :]) + HY_SHIFT
    mirror = jnp.concatenate([jnp.zeros((1,), I32), jnp.arange(seq - 1, 0, -1, dtype=I32)])
    zz = jnp.concatenate([z, z[mirror]], axis=0)
    emb = z.shape[1]
    emb_pad = ((emb + 63) // 64) * 64
    zz = jnp.pad(zz, ((0, 0), (0, emb_pad - emb)))
    zero = jnp.zeros_like(window)
    first = (jnp.arange(seq) == 0)[:, None]
    wf = jnp.concatenate([window, zero], axis=0)
    wb = jnp.concatenate([jnp.where(first, window, 0.0),
                          jnp.where(first, 0.0, window[mirror])], axis=0)
    return zz, wf, wb


def _hyena(proj3, p, l, dft, tabs, col0, width):
    bsz, seq, _ = proj3.shape
    m1_data, m1_filt, m2, g, gi, n2 = dft
    nh = n2 // 2
    cols = LANES * width
    pairs = bsz // 2
    emb_pad = tabs[0].shape[1]
    w1p = jnp.pad(p['hf_w1'][l], ((0, emb_pad - p['hf_w1'].shape[1]), (0, 0)))
    hid = p['hf_w4'].shape[1]
    w4 = p['hf_w4'][l].reshape(hid, 2, 2, width)
    w4f = w4[:, :, 0, :].reshape(hid, 2 * width)
    w4b = w4[:, :, 1, :].reshape(hid, 2 * width)
    kern = _hy_kernels(tabs, w1p, p['hf_b1'][l], p['hf_w2'][l], p['hf_b2'][l], p['hf_w3'][l], p['hf_b3'][l],
                       p['hf_freq'][l], w4f, w4b, seq, width)
    ka = _fft_a(m1_filt, kern[0].reshape(2, nh, cols), kern[1].reshape(2, nh, cols), 2,
                lambda q: q, lambda q: q)
    kf = _fft_bf(ka.reshape(2, 2 * n2, LANES, width), g, n2)

    vxx = _hy_short(proj3, p['hy_conv_w'][l], p['hy_conv_b'][l], col0, width)
    z = vxx[0].reshape(bsz, nh, cols)
    for order in range(2):
        a = _fft_a(m1_data, z, z, pairs, lambda q: 2 * q, lambda q: 2 * q + 1)
        vre, vim = _fft_b(a.reshape(pairs, 2 * n2, LANES, width), g, gi, kf[order], n2)
        dbias_t = jnp.tile(p['hy_bias'][l, order], LANES).reshape(1, cols)
        z = _fft_c(m2, vre.reshape(pairs, n2, cols), vim.reshape(pairs, n2, cols),
                   vxx[1 + order].reshape(bsz, nh, cols), z, dbias_t)
    return z.reshape(bsz, seq, width)


CF_ROWS = 64


def _conf_kernel(ap_ref, a_ref, an_ref, gp_ref, g_ref, gn_ref, w_ref, b_ref, lg_ref, lb_ref,
                 o_ref, u_scr, acc_scr):
    i = pl.program_id(1)
    last = pl.num_programs(1) - 1
    tl = a_ref.shape[1]
    width = a_ref.shape[2]

    def glu(a, g):
        return a.astype(F32) * jax.nn.sigmoid(g.astype(F32))

    u_scr[0:CF_HALO] = jnp.where(i > 0, glu(ap_ref[0], gp_ref[0]), 0.0)
    u_scr[CF_HALO:CF_HALO + tl] = glu(a_ref[0], g_ref[0])
    u_scr[CF_HALO + tl:2 * CF_HALO + tl] = jnp.where(i < last, glu(an_ref[0], gn_ref[0]), 0.0)

    shift = CF_HALO - CF_KERNEL // 2

    def rows(r, carry):
        r0 = pl.multiple_of(r * CF_ROWS, CF_ROWS)
        for c in range(width // LANES):
            cs = slice(c * LANES, (c + 1) * LANES)
            win = u_scr[pl.ds(r0, CF_ROWS + 2 * CF_HALO), cs]
            acc = jnp.zeros((CF_ROWS, LANES), F32)
            for j in range(CF_KERNEL):
                acc = acc + w_ref[j:j + 1, cs] * win[shift + j:shift + j + CF_ROWS]
            acc_scr[pl.ds(r0, CF_ROWS), cs] = acc
        return carry

    lax.fori_loop(0, tl // CF_ROWS, rows, 0)
    y = acc_scr[...] + b_ref[...]
    mu = jnp.mean(y, axis=-1, keepdims=True)
    yc = y - mu
    var = jnp.mean(yc * yc, axis=-1, keepdims=True)
    yn = yc * lax.rsqrt(var + EPS) * lg_ref[...] + lb_ref[...]
    o_ref[0] = (yn * jax.nn.sigmoid(yn)).astype(o_ref.dtype)


def _conformer(proj3, dw_w, dw_b, ln_g, ln_b, col0, width):
    bsz, seq, _ = proj3.shape
    tl = _tile(seq, 512)
    a_blk = col0 // width
    g_blk = a_blk + 1
    hpt = tl // CF_HALO
    nhalo = seq // CF_HALO

    def cur(blk):
        return pl.BlockSpec((1, tl, width), lambda b, i: (b, i, blk))

    def prev(blk):
        return pl.BlockSpec((1, CF_HALO, width), lambda b, i: (b, jnp.maximum(i * hpt - 1, 0), blk))

    def nxt(blk):
        return pl.BlockSpec((1, CF_HALO, width), lambda b, i: (b, jnp.minimum((i + 1) * hpt, nhalo - 1), blk))

    vec = lambda rows: pl.BlockSpec((rows, width), lambda b, i: (0, 0))
    return pl.pallas_call(
        _conf_kernel,
        grid=(bsz, seq // tl),
        in_specs=[prev(a_blk), cur(a_blk), nxt(a_blk), prev(g_blk), cur(g_blk), nxt(g_blk),
                  vec(CF_KERNEL), vec(1), vec(1), vec(1)],
        out_specs=pl.BlockSpec((1, tl, width), lambda b, i: (b, i, 0)),
        out_shape=jax.ShapeDtypeStruct((bsz, seq, width), BF16),
        scratch_shapes=[pltpu.VMEM((tl + 2 * CF_HALO, width), F32), pltpu.VMEM((tl, width), F32)],
        compiler_params=_cparams("arbitrary", "arbitrary"),
        name="conformer",
    )(proj3, proj3, proj3, proj3, proj3, proj3, dw_w, dw_b.reshape(1, width),
      ln_g.reshape(1, width), ln_b.reshape(1, width))


def _merge_kernel(oa_ref, oh_ref, oc_ref, ga_ref, gh_ref, gc_ref, wb_ref, wo_ref, x_ref, g1_ref,
                  n2_ref, sh2_ref, sc2_ref, wr_ref, br_ref, xo_ref, h2_ref, lg_ref):
    merged = None
    for n, (o_ref, gt_ref) in enumerate(((oa_ref, ga_ref), (oh_ref, gh_ref), (oc_ref, gc_ref))):
        br = jnp.dot(o_ref[...], wb_ref[n], preferred_element_type=F32)
        term = br * jax.nn.sigmoid(gt_ref[...].astype(F32))
        merged = term if merged is None else merged + term
    y = jnp.dot(merged.astype(BF16), wo_ref[...], preferred_element_type=F32)
    xn = x_ref[...] + g1_ref[0] * y
    xo_ref[...] = xn
    hn = xn * lax.rsqrt(jnp.mean(xn * xn, axis=-1, keepdims=True) + EPS)
    h2 = (hn * n2_ref[...]) * (1.0 + sc2_ref[0]) + sh2_ref[0]
    h2_ref[...] = h2
    lg_ref[...] = jnp.dot(h2, wr_ref[...], precision=lax.Precision.HIGHEST,
                          preferred_element_type=F32) + br_ref[...]


def _merge(o_att, o_hy, o_cf, proj2, gate_col0, wb_bf, wo_bf, x2, mod3, norm2_g, wr_pad, br_pad, seq):
    t, d = x2.shape
    bw = o_att.shape[1]
    tm = _tile(seq, 512)
    per_b = seq // tm
    gb = gate_col0 // d
    epad = wr_pad.shape[1]
    row = lambda w: pl.BlockSpec((tm, w), lambda i: (i, 0))
    modc = lambda c: pl.BlockSpec((1, 1, d), lambda i: (i // per_b, 0, c))
    return pl.pallas_call(
        _merge_kernel,
        grid=(t // tm,),
        in_specs=[row(bw), row(bw), row(bw),
                  pl.BlockSpec((tm, d), lambda i: (i, gb)),
                  pl.BlockSpec((tm, d), lambda i: (i, gb + 1)),
                  pl.BlockSpec((tm, d), lambda i: (i, gb + 2)),
                  pl.BlockSpec((3, bw, d), lambda i: (0, 0, 0)),
                  pl.BlockSpec((d, d), lambda i: (0, 0)),
                  row(d),
                  modc(2),
                  pl.BlockSpec((1, d), lambda i: (0, 0)),
                  modc(3), modc(4),
                  pl.BlockSpec((d, epad), lambda i: (0, 0)),
                  pl.BlockSpec((1, epad), lambda i: (0, 0))],
        out_specs=[row(d), row(d), row(epad)],
        out_shape=[jax.ShapeDtypeStruct((t, d), F32), jax.ShapeDtypeStruct((t, d), F32),
                   jax.ShapeDtypeStruct((t, epad), F32)],
        compiler_params=_cparams("arbitrary"),
        name="merge",
    )(o_att, o_hy, o_cf, proj2, proj2, proj2, wb_bf, wo_bf, x2, mod3, norm2_g.reshape(1, d),
      mod3, mod3, wr_pad, br_pad)


def _router_kernel(lg_ref, tri_ref, idx_ref, w_ref, rank_ref, cnt_ref, carry_scr, *, n_exp):
    @pl.when(pl.program_id(0) == 0)
    def _():
        carry_scr[...] = jnp.zeros_like(carry_scr)

    lt = lg_ref[...].T[:n_exp]
    eio = lax.broadcasted_iota(I32, lt.shape, 0)
    cur = lt
    vals, idxs = [], []
    for _ in range(TOP_K):
        m = jnp.max(cur, axis=0, keepdims=True)
        ik = jnp.min(jnp.where(cur == m, eio, n_exp), axis=0, keepdims=True)
        vals.append(m)
        idxs.append(ik)
        cur = jnp.where(eio == ik, -jnp.inf, cur)
    ex = [jnp.exp(v - vals[0]) for v in vals]
    den = ex[0] + ex[1] + ex[2] + ex[3]
    onehot = jnp.zeros(lt.shape, F32)
    for ik in idxs:
        onehot = onehot + jnp.where(eio == ik, 1.0, 0.0)
    incl = jnp.dot(onehot.astype(BF16), tri_ref[...], preferred_element_type=F32)
    carry = carry_scr[...]
    before = incl - onehot + carry[:, 0:1]
    for k in range(TOP_K):
        idx_ref[k:k + 1, :] = idxs[k]
        w_ref[k:k + 1, :] = ex[k] / den
        rank_ref[k:k + 1, :] = jnp.sum(jnp.where(eio == idxs[k], before, 0.0), axis=0,
                                       keepdims=True).astype(I32)
    new_carry = carry + incl[:, incl.shape[1] - 1:]
    carry_scr[...] = new_carry
    cnt_ref[...] = new_carry


def _router(logits_pad, n_exp):
    t, epad = logits_pad.shape
    tm = _tile(t, 1024)
    tri = (jnp.arange(tm)[:, None] <= jnp.arange(tm)[None, :]).astype(BF16)
    out_i = jax.ShapeDtypeStruct((TOP_K, t), I32)
    return pl.pallas_call(
        functools.partial(_router_kernel, n_exp=n_exp),
        grid=(t // tm,),
        in_specs=[pl.BlockSpec((tm, epad), lambda i: (i, 0)),
                  pl.BlockSpec((tm, tm), lambda i: (0, 0))],
        out_specs=[pl.BlockSpec((TOP_K, tm), lambda i: (0, i)),
                   pl.BlockSpec((TOP_K, tm), lambda i: (0, i)),
                   pl.BlockSpec((TOP_K, tm), lambda i: (0, i)),
                   pl.BlockSpec((n_exp, LANES), lambda i: (0, 0))],
        out_shape=[out_i, jax.ShapeDtypeStruct((TOP_K, t), F32), out_i,
                   jax.ShapeDtypeStruct((n_exp, LANES), F32)],
        scratch_shapes=[pltpu.VMEM((n_exp, LANES), F32)],
        compiler_params=_cparams("arbitrary"),
        name="router",
    )(logits_pad, tri)


def _gather_rows(idx_ref, n_rows, src_hbm, dst, sem):
    def body(r, carry):
        pltpu.make_async_copy(src_hbm.at[pl.ds(idx_ref[0, 0, r], 1)], dst.at[pl.ds(r, 1)], sem).start()
        return carry
    lax.fori_loop(0, n_rows, body, 0)


def _wait_rows(n_rows, src_hbm, dst, sem):
    pltpu.make_async_copy(src_hbm.at[pl.ds(0, n_rows)], dst, sem).wait()


def _expert_kernel(te_ref, tv_ref, tokc_ref, tokn_ref, h_hbm, wgu_ref, bgu_ref, wdn_ref, bdn_ref,
                   o_ref, xbuf, sem):
    del te_ref
    i = pl.program_id(0)
    n = pl.num_programs(0)
    tm = o_ref.shape[0]
    ff = wdn_ref.shape[1]
    slot = i % 2

    @pl.when(jnp.logical_and(i == 0, tv_ref[0] == 1))
    def _():
        _gather_rows(tokc_ref, tm, h_hbm, xbuf.at[0], sem.at[0])

    @pl.when(jnp.logical_and(i + 1 < n, tv_ref[jnp.minimum(i + 1, n - 1)] == 1))
    def _():
        _gather_rows(tokn_ref, tm, h_hbm, xbuf.at[1 - slot], sem.at[1 - slot])

    @pl.when(tv_ref[i] == 1)
    def _():
        _wait_rows(tm, h_hbm, xbuf.at[slot], sem.at[slot])
        xb = xbuf[slot].astype(BF16)
        gu = jnp.dot(xb, wgu_ref[0], preferred_element_type=F32) + bgu_ref[0]
        g = jnp.minimum(gu[:, :ff], SWIGLU_LIMIT)
        u = jnp.clip(gu[:, ff:], -SWIGLU_LIMIT, SWIGLU_LIMIT)
        hd = (u + 1.0) * (g * jax.nn.sigmoid(SWIGLU_ALPHA * g))
        y = jnp.dot(hd.astype(BF16), wdn_ref[0], preferred_element_type=F32) + bdn_ref[0]
        o_ref[...] = y.astype(o_ref.dtype)

    @pl.when(tv_ref[i] == 0)
    def _():
        o_ref[...] = jnp.zeros_like(o_ref)


def _experts(h2, row_token3, tile_e, tile_v, wgu_bf, bgu, wdn_bf, bdn, tm):
    n_tiles = row_token3.shape[0]
    n_exp, d, ff2 = wgu_bf.shape
    ff = ff2 // 2
    grid_spec = pltpu.PrefetchScalarGridSpec(
        num_scalar_prefetch=2,
        grid=(n_tiles,),
        in_specs=[pl.BlockSpec((1, 1, tm), lambda i, te, tv: (i, 0, 0), memory_space=pltpu.SMEM),
                  pl.BlockSpec((1, 1, tm), lambda i, te, tv: (jnp.minimum(i + 1, n_tiles - 1), 0, 0),
                               memory_space=pltpu.SMEM),
                  pl.BlockSpec(memory_space=pl.ANY),
                  pl.BlockSpec((1, d, ff2), lambda i, te, tv: (te[i], 0, 0)),
                  pl.BlockSpec((1, 1, ff2), lambda i, te, tv: (te[i], 0, 0)),
                  pl.BlockSpec((1, ff, d), lambda i, te, tv: (te[i], 0, 0)),
                  pl.BlockSpec((1, 1, d), lambda i, te, tv: (te[i], 0, 0))],
        out_specs=pl.BlockSpec((tm, d), lambda i, te, tv: (i, 0)),
        scratch_shapes=[pltpu.VMEM((2, tm, d), F32), pltpu.SemaphoreType.DMA((2,))],
    )
    return pl.pallas_call(
        _expert_kernel,
        grid_spec=grid_spec,
        out_shape=jax.ShapeDtypeStruct((n_tiles * tm, d), F32),
        compiler_params=_cparams("arbitrary"),
        name="experts",
    )(tile_e, tile_v, row_token3, row_token3, h2, wgu_bf, bgu.reshape(n_exp, 1, ff2),
      wdn_bf, bdn.reshape(n_exp, 1, d))


def _combine_kernel(posc_ref, posn_ref, ys_hbm, w_ref, x_ref, g2_ref, o_ref, buf, sem):
    i = pl.program_id(0)
    n = pl.num_programs(0)
    tc = x_ref.shape[0]
    rows = TOP_K * tc
    slot = i % 2

    @pl.when(i == 0)
    def _():
        _gather_rows(posc_ref, rows, ys_hbm, buf.at[0], sem.at[0])

    @pl.when(i + 1 < n)
    def _():
        _gather_rows(posn_ref, rows, ys_hbm, buf.at[1 - slot], sem.at[1 - slot])

    _wait_rows(rows, ys_hbm, buf.at[slot], sem.at[slot])
    w = w_ref[...]
    acc = w[:, 0:1] * buf[slot, 0:tc]
    for k in range(1, TOP_K):
        acc = acc + w[:, k:k + 1] * buf[slot, k * tc:(k + 1) * tc]
    o_ref[...] = x_ref[...] + g2_ref[0] * acc


def _combine(ys, pos3, w_t, x2, mod3, seq):
    t, d = x2.shape
    n_tok_tiles, _, rows = pos3.shape
    tc = rows // TOP_K
    per_b = seq // tc
    return pl.pallas_call(
        _combine_kernel,
        grid=(n_tok_tiles,),
        in_specs=[pl.BlockSpec((1, 1, rows), lambda i: (i, 0, 0), memory_space=pltpu.SMEM),
                  pl.BlockSpec((1, 1, rows), lambda i: (jnp.minimum(i + 1, n_tok_tiles - 1), 0, 0),
                               memory_space=pltpu.SMEM),
                  pl.BlockSpec(memory_space=pl.ANY),
                  pl.BlockSpec((tc, TOP_K), lambda i: (i, 0)),
                  pl.BlockSpec((tc, d), lambda i: (i, 0)),
                  pl.BlockSpec((1, 1, d), lambda i: (i // per_b, 0, 5))],
        out_specs=pl.BlockSpec((tc, d), lambda i: (i, 0)),
        out_shape=jax.ShapeDtypeStruct((t, d), F32),
        scratch_shapes=[pltpu.VMEM((2, rows, d), F32), pltpu.SemaphoreType.DMA((2,))],
        compiler_params=_cparams("arbitrary"),
        name="combine",
    )(pos3, pos3, ys, w_t, x2, mod3)


def _moe(h2, logits_pad, x2, mod3, wgu_bf, bgu, wdn_bf, bdn, seq):
    t, d = x2.shape
    n_exp = wgu_bf.shape[0]
    idx, w, rank, cnt = _router(logits_pad, n_exp)
    tm = _tile(t * TOP_K // n_exp, 512)
    tc = _tile(seq, 256)
    n_tiles = (t * TOP_K) // tm + n_exp
    counts = cnt[:, 0].astype(I32)
    gsz = ((counts + tm - 1) // tm) * tm
    gend = jnp.cumsum(gsz)
    goff = gend - gsz
    pos = goff[idx] + rank
    tile_start = jnp.arange(n_tiles, dtype=I32) * tm
    tile_v = (tile_start < gend[-1]).astype(I32)
    tile_e = jnp.minimum(jnp.searchsorted(gend, tile_start, side='right'), n_exp - 1).astype(I32)
    tok = jnp.broadcast_to(jnp.arange(t, dtype=I32)[None, :], (TOP_K, t))
    row_token = jnp.zeros((n_tiles * tm,), I32).at[pos.reshape(-1)].set(tok.reshape(-1))
    ys = _experts(h2, row_token.reshape(n_tiles, 1, tm), tile_e, tile_v, wgu_bf, bgu, wdn_bf, bdn, tm)
    pos3 = pos.reshape(TOP_K, t // tc, tc).transpose(1, 0, 2).reshape(t // tc, 1, TOP_K * tc)
    return _combine(ys, pos3, w.T, x2, mod3, seq)


def kernel(x, c, norm1_g, norm2_g, w_ada, b_ada, w_in, b_in, q_norm_g, k_norm_g, lam_q1, lam_k1, lam_q2, lam_k2, subln_g, hy_conv_w, hy_conv_b, hf_w1, hf_b1, hf_w2, hf_b2, hf_w3, hf_b3, hf_freq, hf_w4, hy_bias, cf_dw_w, cf_dw_b, cf_ln_g, cf_ln_b, w_branch, w_out, w_router, b_router, w_gu, b_gu, w_down, b_down):
    bsz, seq, d = x.shape
    depth = w_ada.shape[0]
    t = bsz * seq
    bw = d // 2
    att_w = ATT_HEADS * 2 * HEAD_DIM
    hy_col0 = 3 * att_w
    cf_col0 = hy_col0 + 3 * bw
    gate_col0 = cf_col0 + 2 * bw
    n_exp = w_router.shape[2]
    assert att_w == bw and bsz % 2 == 0 and seq % LANES == 0

    hp = dict(hf_w1=hf_w1, hf_b1=hf_b1, hf_w2=hf_w2, hf_b2=hf_b2, hf_w3=hf_w3, hf_b3=hf_b3,
              hf_freq=hf_freq, hf_w4=hf_w4, hy_conv_w=hy_conv_w, hy_conv_b=hy_conv_b, hy_bias=hy_bias)

    pos = jnp.arange(seq, dtype=F32)
    inv = 1.0 / (ROPE_THETA ** (jnp.arange(0, HEAD_DIM, 2, dtype=F32) / HEAD_DIM))
    ang = pos[:, None] * inv[None, :]
    ang = jnp.concatenate([ang, ang], axis=-1)
    sign = jnp.where(jnp.arange(HEAD_DIM) < HEAD_DIM // 2, -1.0, 1.0).astype(F32)
    reps = att_w // HEAD_DIM
    cos_t = jnp.tile(jnp.cos(ang), (1, reps))
    sin_t = jnp.tile(jnp.sin(ang) * sign[None, :], (1, reps))
    lane = jnp.arange(att_w)
    gsum = (lane[:, None] // HEAD_DIM == lane[None, :] // HEAD_DIM).astype(BF16)
    dft = _dft_tables(seq)
    tabs = _hy_tables(seq, bw)
    epad = LANES

    mod_all = _ada_all(c, w_ada, b_ada)
    x2 = x.reshape(t, d)
    for l in range(depth):
        mod3 = mod_all[l].reshape(bsz, 1, 6 * d)
        lam_init = 0.8 - 0.6 * math.exp(-0.3 * l)

        proj2 = _in_proj(x2, mod3, norm1_g[l], w_in[l].astype(BF16), b_in[l], seq)
        proj3 = proj2.reshape(bsz, seq, -1)

        qr, kt = _qk_prep(proj3, cos_t, sin_t, jnp.tile(q_norm_g[l], reps).reshape(1, att_w),
                          jnp.tile(k_norm_g[l], reps).reshape(1, att_w), gsum)
        lamv = jnp.stack([lam_q1[l], lam_k1[l], lam_q2[l], lam_k2[l]])
        o_att = _attention(qr, kt, proj3, lamv, subln_g[l], lam_init)

        o_hy = _hyena(proj3, hp, l, dft, tabs, hy_col0, bw)
        o_cf = _conformer(proj3, cf_dw_w[l], cf_dw_b[l], cf_ln_g[l], cf_ln_b[l], cf_col0, bw)

        wr_pad = jnp.pad(w_router[l], ((0, 0), (0, epad - n_exp)))
        br_pad = jnp.pad(b_router[l], (0, epad - n_exp)).reshape(1, epad)
        x2, h2, logits = _merge(o_att.reshape(t, bw), o_hy.reshape(t, bw), o_cf.reshape(t, bw), proj2,
                                gate_col0, w_branch[l].astype(BF16), w_out[l].astype(BF16), x2, mod3,
                                norm2_g[l], wr_pad, br_pad, seq)
        x2 = _moe(h2, logits, x2, mod3, w_gu[l].astype(BF16), b_gu[l], w_down[l].astype(BF16),
                  b_down[l], seq)
    return x2.reshape(bsz, seq, d)
```

```python
import functools
import math

import jax
import jax.numpy as jnp
from jax import lax
from jax.experimental import pallas as pl
from jax.experimental.pallas import tpu as pltpu

F32 = jnp.float32
BF16 = jnp.bfloat16
I32 = jnp.int32

LANES = 128
VMEM_LIMIT_BYTES = 56 * 1024 * 1024

ATT_HEADS = 4
HEAD_DIM = 64
V_DIM = 2 * HEAD_DIM
ROPE_THETA = 10000.0
HY_BANDS = 16
HY_FAST_DECAY = 0.3
HY_SLOW_DECAY = 1.5
HY_TARGET = 1e-2
HY_SHIFT = 0.05
CF_KERNEL = 31
CF_HALO = 16
TOP_K = 4
SWIGLU_LIMIT = 7.0
SWIGLU_ALPHA = 1.702
EPS = 1e-6
LOG2E = 1.4426950408889634


def _cparams(*sem):
    return pltpu.CompilerParams(dimension_semantics=sem, vmem_limit_bytes=VMEM_LIMIT_BYTES)


def _tile(n, want):
    t = min(n, want)
    while n % t:
        t -= 1
    return t


def _ada_kernel(c_ref, w_ref, b_ref, o_ref):
    c = c_ref[...]
    cond = c * jax.nn.sigmoid(c)
    o_ref[0] = jnp.dot(cond.astype(BF16), w_ref[0].astype(BF16),
                       preferred_element_type=F32) + b_ref[0]


def _ada_all(c, w_ada, b_ada):
    depth, d, n = w_ada.shape
    bsz = c.shape[0]
    tn = _tile(n, 1536)
    return pl.pallas_call(
        _ada_kernel,
        grid=(depth, n // tn),
        in_specs=[pl.BlockSpec((bsz, d), lambda l, j: (0, 0)),
                  pl.BlockSpec((1, d, tn), lambda l, j: (l, 0, j)),
                  pl.BlockSpec((1, 1, tn), lambda l, j: (l, 0, j))],
        out_specs=pl.BlockSpec((1, bsz, tn), lambda l, j: (l, 0, j)),
        out_shape=jax.ShapeDtypeStruct((depth, bsz, n), F32),
        compiler_params=_cparams("arbitrary", "arbitrary"),
        name="ada_mod",
    )(c, w_ada, b_ada.reshape(depth, 1, n))


def _proj_kernel(x_ref, sh_ref, sc_ref, g_ref, w_ref, b_ref, o_ref, h_scr):
    @pl.when(pl.program_id(1) == 0)
    def _():
        x = x_ref[...]
        y = x * lax.rsqrt(jnp.mean(x * x, axis=-1, keepdims=True) + EPS)
        h = (y * g_ref[...]) * (1.0 + sc_ref[0]) + sh_ref[0]
        h_scr[...] = h.astype(BF16)

    acc = jnp.dot(h_scr[...], w_ref[...], preferred_element_type=F32)
    o_ref[...] = (acc + b_ref[...]).astype(o_ref.dtype)


def _in_proj(x2, mod3, norm_g, w_bf, b, seq):
    t, d = x2.shape
    n = w_bf.shape[1]
    tm = _tile(seq, 1024)
    tn = _tile(n, 1792)
    per_b = seq // tm
    return pl.pallas_call(
        _proj_kernel,
        grid=(t // tm, n // tn),
        in_specs=[pl.BlockSpec((tm, d), lambda i, j: (i, 0)),
                  pl.BlockSpec((1, 1, d), lambda i, j: (i // per_b, 0, 0)),
                  pl.BlockSpec((1, 1, d), lambda i, j: (i // per_b, 0, 1)),
                  pl.BlockSpec((1, d), lambda i, j: (0, 0)),
                  pl.BlockSpec((d, tn), lambda i, j: (0, j)),
                  pl.BlockSpec((1, tn), lambda i, j: (0, j))],
        out_specs=pl.BlockSpec((tm, tn), lambda i, j: (i, j)),
        out_shape=jax.ShapeDtypeStruct((t, n), BF16),
        scratch_shapes=[pltpu.VMEM((tm, d), BF16)],
        compiler_params=_cparams("arbitrary", "arbitrary"),
        name="in_proj",
    )(x2, mod3, mod3, norm_g.reshape(1, d), w_bf, b.reshape(1, n))


def _qkprep_kernel(q_ref, k_ref, cos_ref, sin_ref, gq_ref, gk_ref, gs_ref, qo_ref, kto_ref, *, q_scale):
    width = q_ref.shape[-1]
    lane = lax.broadcasted_iota(I32, q_ref.shape[1:], 1)
    first_half = (lane % HEAD_DIM) < (HEAD_DIM // 2)

    def prep(t, g):
        ssq = jnp.dot((t * t).astype(BF16), gs_ref[...], preferred_element_type=F32)
        y = t * lax.rsqrt(ssq * (1.0 / HEAD_DIM) + EPS) * g
        rot = jnp.where(first_half,
                        pltpu.roll(y, width - HEAD_DIM // 2, 1),
                        pltpu.roll(y, HEAD_DIM // 2, 1))
        return y * cos_ref[...] + rot * sin_ref[...]

    q = prep(q_ref[0].astype(F32), gq_ref[...])
    qo_ref[0] = (q * q_scale).astype(BF16)
    k = prep(k_ref[0].astype(F32), gk_ref[...])
    kto_ref[0] = k.T.astype(BF16)


def _qk_prep(proj3, cos_t, sin_t, gq, gk, gsum):
    bsz, seq, _ = proj3.shape
    width = ATT_HEADS * 2 * HEAD_DIM
    tr = _tile(seq, 512)
    q_scale = HEAD_DIM ** -0.5 * LOG2E
    return pl.pallas_call(
        functools.partial(_qkprep_kernel, q_scale=q_scale),
        grid=(bsz, seq // tr),
        in_specs=[pl.BlockSpec((1, tr, width), lambda b, i: (b, i, 0)),
                  pl.BlockSpec((1, tr, width), lambda b, i: (b, i, 1)),
                  pl.BlockSpec((tr, width), lambda b, i: (i, 0)),
                  pl.BlockSpec((tr, width), lambda b, i: (i, 0)),
                  pl.BlockSpec((1, width), lambda b, i: (0, 0)),
                  pl.BlockSpec((1, width), lambda b, i: (0, 0)),
                  pl.BlockSpec((width, width), lambda b, i: (0, 0))],
        out_specs=[pl.BlockSpec((1, tr, width), lambda b, i: (b, i, 0)),
                   pl.BlockSpec((1, width, tr), lambda b, i: (b, 0, i))],
        out_shape=[jax.ShapeDtypeStruct((bsz, seq, width), BF16),
                   jax.ShapeDtypeStruct((bsz, width, seq), BF16)],
        compiler_params=_cparams("arbitrary", "arbitrary"),
        name="qk_prep",
    )(proj3, proj3, cos_t, sin_t, gq, gk, gsum)


def _attn_kernel(q_ref, kt_ref, v_ref, lamv_ref, g_ref, o_ref, *, lam_init):
    lamv = lamv_ref[...]
    lam = (jnp.exp(jnp.sum(lamv[0:1] * lamv[1:2], axis=-1, keepdims=True))
           - jnp.exp(jnp.sum(lamv[2:3] * lamv[3:4], axis=-1, keepdims=True)) + lam_init)
    q = q_ref[0]
    seq = v_ref.shape[1]
    ones = jnp.ones((seq, V_DIM), BF16)
    for h in range(ATT_HEADS):
        v_ext = jnp.concatenate([v_ref[0, :, h * V_DIM:(h + 1) * V_DIM], ones], axis=1)
        outs = []
        for comp in range(2):
            lo = (2 * h + comp) * HEAD_DIM
            s = jnp.dot(q[:, lo:lo + HEAD_DIM], kt_ref[0, lo:lo + HEAD_DIM, :],
                        preferred_element_type=F32)
            p = jnp.exp2((s - jnp.max(s, axis=-1, keepdims=True)).astype(BF16))
            ov = jnp.dot(p, v_ext, preferred_element_type=F32)
            outs.append(ov[:, :V_DIM] * (1.0 / ov[:, V_DIM:V_DIM + 1]))
        o = outs[0] - lam * outs[1]
        o = o * lax.rsqrt(jnp.mean(o * o, axis=-1, keepdims=True) + EPS)
        o_ref[0, :, h * V_DIM:(h + 1) * V_DIM] = (o * g_ref[...] * (1.0 - lam_init)).astype(o_ref.dtype)


def _attention(qr, kt, proj3, lamv, subln_g, lam_init):
    bsz, seq, width = qr.shape
    tq = _tile(seq, 256)
    return pl.pallas_call(
        functools.partial(_attn_kernel, lam_init=lam_init),
        grid=(bsz, seq // tq),
        in_specs=[pl.BlockSpec((1, tq, width), lambda b, i: (b, i, 0)),
                  pl.BlockSpec((1, width, seq), lambda b, i: (b, 0, 0)),
                  pl.BlockSpec((1, seq, width), lambda b, i: (b, 0, 2)),
                  pl.BlockSpec((4, HEAD_DIM), lambda b, i: (0, 0)),
                  pl.BlockSpec((1, V_DIM), lambda b, i: (0, 0))],
        out_specs=pl.BlockSpec((1, tq, width), lambda b, i: (b, i, 0)),
        out_shape=jax.ShapeDtypeStruct((bsz, seq, width), BF16),
        compiler_params=_cparams("arbitrary", "arbitrary"),
        name="diff_attn",
    )(qr, kt, proj3, lamv, subln_g.reshape(1, V_DIM))


def _hyshort_kernel(u_ref, w_ref, b_ref, o_ref):
    u = u_ref[0].astype(F32)
    seq = u.shape[0]
    row = lax.broadcasted_iota(I32, u.shape, 0)
    prev = jnp.where(row == 0, 0.0, pltpu.roll(u, 1, 0))
    nxt = jnp.where(row == seq - 1, 0.0, pltpu.roll(u, seq - 1, 0))
    w = w_ref[...]
    y = w[0:1] * prev + w[1:2] * u + w[2:3] * nxt + b_ref[...]
    o_ref[0, 0] = y.astype(o_ref.dtype)


def _hy_short(proj3, conv_w, conv_b, col0, width):
    bsz, seq, _ = proj3.shape
    cb = LANES
    per = width // cb
    return pl.pallas_call(
        _hyshort_kernel,
        grid=(bsz, 3 * per),
        in_specs=[pl.BlockSpec((1, seq, cb), lambda b, j: (b, 0, col0 // cb + j)),
                  pl.BlockSpec((3, cb), lambda b, j: (0, j)),
                  pl.BlockSpec((1, cb), lambda b, j: (0, j))],
        out_specs=pl.BlockSpec((1, 1, seq, cb), lambda b, j: (j // per, b, 0, j % per)),
        out_shape=jax.ShapeDtypeStruct((3, bsz, seq, width), BF16),
        compiler_params=_cparams("arbitrary", "arbitrary"),
        name="hy_short",
    )(proj3, conv_w, conv_b.reshape(1, 3 * width))


def _filt_kernel(zz_ref, w1_ref, b1_ref, w2_ref, b2_ref, w3_ref, b3_ref, fr_ref,
                 w4f_ref, w4b_ref, wf_ref, wb_ref, o_ref):
    hi = lax.Precision.HIGHEST
    fr = fr_ref[...]
    h = jnp.sin(fr * (jnp.dot(zz_ref[...], w1_ref[...], precision=hi, preferred_element_type=F32) + b1_ref[...]))
    h = jnp.sin(fr * (jnp.dot(h, w2_ref[...], precision=hi, preferred_element_type=F32) + b2_ref[...]))
    h = jnp.sin(fr * (jnp.dot(h, w3_ref[...], precision=hi, preferred_element_type=F32) + b3_ref[...]))
    ff = jnp.dot(h, w4f_ref[...], precision=hi, preferred_element_type=F32)
    fb = jnp.dot(h, w4b_ref[...], precision=hi, preferred_element_type=F32)
    width = wf_ref.shape[-1]
    wf = wf_ref[...]
    wb = wb_ref[...]
    for n in range(2):
        sl = slice(n * width, (n + 1) * width)
        o_ref[0, n] = (ff[:, sl] * wf + fb[:, sl] * wb).astype(o_ref.dtype)


def _hy_kernels(tabs, w1p, b1, w2, b2, w3, b3, freq, w4f, w4b, seq, width):
    zz, wf, wb = tabs
    hid = w2.shape[0]
    tr = _tile(seq, 512)
    nt = seq // tr
    emb = zz.shape[1]
    full = lambda shape: pl.BlockSpec(shape, lambda hf, i: tuple(0 for _ in shape))
    return pl.pallas_call(
        _filt_kernel,
        grid=(2, nt),
        in_specs=[pl.BlockSpec((tr, emb), lambda hf, i: (hf * nt + i, 0)),
                  full((emb, hid)), full((1, hid)), full((hid, hid)), full((1, hid)),
                  full((hid, hid)), full((1, hid)), full((1, hid)),
                  full((hid, 2 * width)), full((hid, 2 * width)),
                  pl.BlockSpec((tr, width), lambda hf, i: (hf * nt + i, 0)),
                  pl.BlockSpec((tr, width), lambda hf, i: (hf * nt + i, 0))],
        out_specs=pl.BlockSpec((1, 2, tr, width), lambda hf, i: (hf, 0, i, 0)),
        out_shape=jax.ShapeDtypeStruct((2, 2, seq, width), BF16),
        compiler_params=_cparams("arbitrary", "arbitrary"),
        name="hy_filter",
    )(zz, w1p, b1.reshape(1, hid), w2, b2.reshape(1, hid), w3, b3.reshape(1, hid),
      freq.reshape(1, hid), w4f, w4b, wf, wb)


def _fft_a_kernel(m_ref, xa_ref, xb_ref, o_ref):
    d = jnp.concatenate([xa_ref[0].astype(BF16), xb_ref[0].astype(BF16)], axis=0)
    o_ref[0] = jnp.dot(m_ref[...], d, preferred_element_type=F32).astype(o_ref.dtype)


def _fft_a(m1, xa, xb, pairs, amap, bmap):
    nh, cols = xa.shape[1], xa.shape[2]
    rows = m1.shape[0]
    tc = _tile(cols, 8192)
    return pl.pallas_call(
        _fft_a_kernel,
        grid=(cols // tc, pairs),
        in_specs=[pl.BlockSpec(m1.shape, lambda j, p: (0, 0)),
                  pl.BlockSpec((1, nh, tc), lambda j, p: (amap(p), 0, j)),
                  pl.BlockSpec((1, nh, tc), lambda j, p: (bmap(p), 0, j))],
        out_specs=pl.BlockSpec((1, rows, tc), lambda j, p: (p, 0, j)),
        out_shape=jax.ShapeDtypeStruct((pairs, rows, cols), BF16),
        compiler_params=_cparams("arbitrary", "arbitrary"),
        name="fft_a",
    )(m1, xa, xb)


def _fft_b_kernel(are_ref, aim_ref, g_ref, gi_ref, kf_ref, vre_ref, vim_ref):
    kb = are_ref.shape[1]
    half = are_ref.shape[2]
    for q in range(kb):
        d = jnp.concatenate([are_ref[0, q], aim_ref[0, q]], axis=0)
        x = jnp.dot(g_ref[q], d, preferred_element_type=F32)
        xr, xi = x[:half], x[half:]
        kr, ki = kf_ref[q, :half], kf_ref[q, half:]
        p = jnp.concatenate([xr * kr - xi * ki, xr * ki + xi * kr], axis=0).astype(BF16)
        v = jnp.dot(gi_ref[q], p, preferred_element_type=F32)
        vre_ref[0, q] = v[:half].astype(vre_ref.dtype)
        vim_ref[0, q] = v[half:].astype(vim_ref.dtype)


def _fft_b(a4, g, gi, kf, n2):
    pairs, _, half, width = a4.shape
    kb = _tile(n2, 8)
    nk = n2 // kb
    return pl.pallas_call(
        _fft_b_kernel,
        grid=(nk, pairs),
        in_specs=[pl.BlockSpec((1, kb, half, width), lambda i, p: (p, i, 0, 0)),
                  pl.BlockSpec((1, kb, half, width), lambda i, p: (p, nk + i, 0, 0)),
                  pl.BlockSpec((kb, 2 * half, 2 * half), lambda i, p: (i, 0, 0)),
                  pl.BlockSpec((kb, 2 * half, 2 * half), lambda i, p: (i, 0, 0)),
                  pl.BlockSpec((kb, 2 * half, width), lambda i, p: (i, 0, 0))],
        out_specs=[pl.BlockSpec((1, kb, half, width), lambda i, p: (p, i, 0, 0)),
                   pl.BlockSpec((1, kb, half, width), lambda i, p: (p, i, 0, 0))],
        out_shape=[jax.ShapeDtypeStruct((pairs, n2, half, width), BF16),
                   jax.ShapeDtypeStruct((pairs, n2, half, width), BF16)],
        compiler_params=_cparams("arbitrary", "arbitrary"),
        name="fft_b",
    )(a4, a4, g, gi, kf)


def _fft_bf_kernel(are_ref, aim_ref, g_ref, kf_ref):
    kb = are_ref.shape[1]
    for q in range(kb):
        d = jnp.concatenate([are_ref[0, q], aim_ref[0, q]], axis=0)
        kf_ref[0, q] = jnp.dot(g_ref[q], d, preferred_element_type=F32)


def _fft_bf(a4, g, n2):
    pairs, _, half, width = a4.shape
    kb = _tile(n2, 8)
    nk = n2 // kb
    return pl.pallas_call(
        _fft_bf_kernel,
        grid=(nk, pairs),
        in_specs=[pl.BlockSpec((1, kb, half, width), lambda i, p: (p, i, 0, 0)),
                  pl.BlockSpec((1, kb, half, width), lambda i, p: (p, nk + i, 0, 0)),
                  pl.BlockSpec((kb, 2 * half, 2 * half), lambda i, p: (i, 0, 0))],
        out_specs=pl.BlockSpec((1, kb, 2 * half, width), lambda i, p: (p, i, 0, 0)),
        out_shape=jax.ShapeDtypeStruct((pairs, n2, 2 * half, width), F32),
        compiler_params=_cparams("arbitrary", "arbitrary"),
        name="fft_bf",
    )(a4, a4, g)


def _fft_c_kernel(m_ref, vre_ref, vim_ref, xg_ref, z_ref, db_ref, o_ref):
    d = jnp.concatenate([vre_ref[0], vim_ref[0]], axis=0)
    y = jnp.dot(m_ref[...], d, preferred_element_type=F32)
    nh = y.shape[0] // 2
    for s in range(2):
        conv = y[s * nh:(s + 1) * nh]
        z = z_ref[s].astype(F32)
        o_ref[s] = (xg_ref[s].astype(F32) * (conv + db_ref[...] * z)).astype(o_ref.dtype)


def _fft_c(m2, vre, vim, xg, z, dbias_t):
    bsz, nh, cols = z.shape
    pairs, n2, _ = vre.shape
    tc = _tile(cols, 8192)
    return pl.pallas_call(
        _fft_c_kernel,
        grid=(cols // tc, pairs),
        in_specs=[pl.BlockSpec(m2.shape, lambda j, p: (0, 0)),
                  pl.BlockSpec((1, n2, tc), lambda j, p: (p, 0, j)),
                  pl.BlockSpec((1, n2, tc), lambda j, p: (p, 0, j)),
                  pl.BlockSpec((2, nh, tc), lambda j, p: (p, 0, j)),
                  pl.BlockSpec((2, nh, tc), lambda j, p: (p, 0, j)),
                  pl.BlockSpec((1, tc), lambda j, p: (0, j))],
        out_specs=pl.BlockSpec((2, nh, tc), lambda j, p: (p, 0, j)),
        out_shape=jax.ShapeDtypeStruct((bsz, nh, cols), BF16),
        compiler_params=_cparams("arbitrary", "arbitrary"),
        name="fft_c",
    )(m2, vre, vim, xg, z, dbias_t)


def _stack_complex(zr, zi):
    return jnp.concatenate([jnp.concatenate([zr, -zi], axis=-1),
                            jnp.concatenate([zi, zr], axis=-1)], axis=-2)


def _dft_tables(seq):
    n = 2 * seq
    n2 = n // LANES
    nh = n2 // 2
    ar = lambda m: jnp.arange(m, dtype=I32)

    def cis(num, den, sign):
        ang = (2.0 * math.pi / den) * (num % den).astype(F32)
        return jnp.cos(ang), sign * jnp.sin(ang)

    fr, fi = cis(ar(n2)[:, None] * ar(n2)[None, :], n2, -1.0)
    m1_data = _stack_complex(fr[:, :nh], fi[:, :nh])
    m1_filt = jnp.concatenate([fr, fi], axis=0)
    m2 = _stack_complex(fr.T[:nh] / n, -fi.T[:nh] / n)
    k2 = ar(n2)[:, None, None]
    k1 = ar(LANES)[None, :, None]
    n1 = ar(LANES)[None, None, :]
    er, ei = cis(n1 * (n2 * k1 + k2), n, -1.0)
    g = _stack_complex(er, ei)
    gi = _stack_complex(jnp.swapaxes(er, 1, 2), -jnp.swapaxes(ei, 1, 2))
    return (m1_data.astype(BF16), m1_filt.astype(BF16), m2.astype(BF16), g.astype(BF16), gi.astype(BF16), n2)


def _hy_tables(seq, width):
    pos = jnp.arange(seq, dtype=F32)
    t = jnp.linspace(0.0, 1.0, seq, dtype=F32)[:, None]
    bands = jnp.linspace(1e-4, HY_BANDS - 1, HY_BANDS, dtype=F32)
    ang = (2.0 * math.pi / seq) * pos[:, None] * bands[None, :]
    z = jnp.concatenate([t, jnp.cos(ang), -jnp.sin(ang)], axis=-1)
    deltas = jnp.abs(jnp.linspace(math.log(HY_TARGET) / HY_FAST_DECAY,
                                  math.log(HY_TARGET) / HY_SLOW_DECAY, width, dtype=F32))
    window = jnp.exp(-t * deltas[None, :]) + HY_SHIFT
    mirror = jnp.concatenate([jnp.zeros((1,), I32), jnp.arange(seq - 1, 0, -1, dtype=I32)])
    zz = jnp.concatenate([z, z[mirror]], axis=0)
    emb = z.shape[1]
    emb_pad = ((emb + 63) // 64) * 64
    zz = jnp.pad(zz, ((0, 0), (0, emb_pad - emb)))
    zero = jnp.zeros_like(window)
    first = (jnp.arange(seq) == 0)[:, None]
    wf = jnp.concatenate([window, zero], axis=0)
    wb = jnp.concatenate([jnp.where(first, window, 0.0),
                          jnp.where(first, 0.0, window[mirror])], axis=0)
    return zz, wf, wb


def _hyena(proj3, p, l, dft, tabs, col0, width):
    bsz, seq, _ = proj3.shape
    m1_data, m1_filt, m2, g, gi, n2 = dft
    nh = n2 // 2
    cols = LANES * width
    pairs = bsz // 2
    emb_pad = tabs[0].shape[1]
    w1p = jnp.pad(p['hf_w1'][l], ((0, emb_pad - p['hf_w1'].shape[1]), (0, 0)))
    hid = p['hf_w4'].shape[1]
    w4 = p['hf_w4'][l].reshape(hid, 2, 2, width)
    w4f = w4[:, :, 0, :].reshape(hid, 2 * width)
    w4b = w4[:, :, 1, :].reshape(hid, 2 * width)
    kern = _hy_kernels(tabs, w1p, p['hf_b1'][l], p['hf_w2'][l], p['hf_b2'][l], p['hf_w3'][l], p['hf_b3'][l],
                       p['hf_freq'][l], w4f, w4b, seq, width)
    ka = _fft_a(m1_filt, kern[0].reshape(2, nh, cols), kern[1].reshape(2, nh, cols), 2,
                lambda q: q, lambda q: q)
    kf = _fft_bf(ka.reshape(2, 2 * n2, LANES, width), g, n2)

    vxx = _hy_short(proj3, p['hy_conv_w'][l], p['hy_conv_b'][l], col0, width)
    z = vxx[0].reshape(bsz, nh, cols)
    for order in range(2):
        a = _fft_a(m1_data, z, z, pairs, lambda q: 2 * q, lambda q: 2 * q + 1)
        vre, vim = _fft_b(a.reshape(pairs, 2 * n2, LANES, width), g, gi, kf[order], n2)
        dbias_t = jnp.tile(p['hy_bias'][l, order], LANES).reshape(1, cols)
        z = _fft_c(m2, vre.reshape(pairs, n2, cols), vim.reshape(pairs, n2, cols),
                   vxx[1 + order].reshape(bsz, nh, cols), z, dbias_t)
    return z.reshape(bsz, seq, width)


CF_ROWS = 64


def _conf_kernel(ap_ref, a_ref, an_ref, gp_ref, g_ref, gn_ref, w_ref, b_ref, lg_ref, lb_ref,
                 o_ref, u_scr, acc_scr):
    i = pl.program_id(1)
    last = pl.num_programs(1) - 1
    tl = a_ref.shape[1]
    width = a_ref.shape[2]

    def glu(a, g):
        return a.astype(F32) * jax.nn.sigmoid(g.astype(F32))

    u_scr[0:CF_HALO] = jnp.where(i > 0, glu(ap_ref[0], gp_ref[0]), 0.0)
    u_scr[CF_HALO:CF_HALO + tl] = glu(a_ref[0], g_ref[0])
    u_scr[CF_HALO + tl:2 * CF_HALO + tl] = jnp.where(i < last, glu(an_ref[0], gn_ref[0]), 0.0)

    shift = CF_HALO - CF_KERNEL // 2

    def rows(r, carry):
        r0 = pl.multiple_of(r * CF_ROWS, CF_ROWS)
        for c in range(width // LANES):
            cs = slice(c * LANES, (c + 1) * LANES)
            win = u_scr[pl.ds(r0, CF_ROWS + 2 * CF_HALO), cs]
            acc = jnp.zeros((CF_ROWS, LANES), F32)
            for j in range(CF_KERNEL):
                acc = acc + w_ref[j:j + 1, cs] * win[shift + j:shift + j + CF_ROWS]
            acc_scr[pl.ds(r0, CF_ROWS), cs] = acc
        return carry

    lax.fori_loop(0, tl // CF_ROWS, rows, 0)
    y = acc_scr[...] + b_ref[...]
    mu = jnp.mean(y, axis=-1, keepdims=True)
    yc = y - mu
    var = jnp.mean(yc * yc, axis=-1, keepdims=True)
    yn = yc * lax.rsqrt(var + EPS) * lg_ref[...] + lb_ref[...]
    o_ref[0] = (yn * jax.nn.sigmoid(yn)).astype(o_ref.dtype)


def _conformer(proj3, dw_w, dw_b, ln_g, ln_b, col0, width):
    bsz, seq, _ = proj3.shape
    tl = _tile(seq, 512)
    a_blk = col0 // width
    g_blk = a_blk + 1
    hpt = tl // CF_HALO
    nhalo = seq // CF_HALO

    def cur(blk):
        return pl.BlockSpec((1, tl, width), lambda b, i: (b, i, blk))

    def prev(blk):
        return pl.BlockSpec((1, CF_HALO, width), lambda b, i: (b, jnp.maximum(i * hpt - 1, 0), blk))

    def nxt(blk):
        return pl.BlockSpec((1, CF_HALO, width), lambda b, i: (b, jnp.minimum((i + 1) * hpt, nhalo - 1), blk))

    vec = lambda rows: pl.BlockSpec((rows, width), lambda b, i: (0, 0))
    return pl.pallas_call(
        _conf_kernel,
        grid=(bsz, seq // tl),
        in_specs=[prev(a_blk), cur(a_blk), nxt(a_blk), prev(g_blk), cur(g_blk), nxt(g_blk),
                  vec(CF_KERNEL), vec(1), vec(1), vec(1)],
        out_specs=pl.BlockSpec((1, tl, width), lambda b, i: (b, i, 0)),
        out_shape=jax.ShapeDtypeStruct((bsz, seq, width), BF16),
        scratch_shapes=[pltpu.VMEM((tl + 2 * CF_HALO, width), F32), pltpu.VMEM((tl, width), F32)],
        compiler_params=_cparams("arbitrary", "arbitrary"),
        name="conformer",
    )(proj3, proj3, proj3, proj3, proj3, proj3, dw_w, dw_b.reshape(1, width),
      ln_g.reshape(1, width), ln_b.reshape(1, width))


def _merge_kernel(oa_ref, oh_ref, oc_ref, ga_ref, gh_ref, gc_ref, wb_ref, wo_ref, x_ref, g1_ref,
                  n2_ref, sh2_ref, sc2_ref, wr_ref, br_ref, xo_ref, h2_ref, lg_ref):
    merged = None
    for n, (o_ref, gt_ref) in enumerate(((oa_ref, ga_ref), (oh_ref, gh_ref), (oc_ref, gc_ref))):
        br = jnp.dot(o_ref[...], wb_ref[n], preferred_element_type=F32)
        term = br * jax.nn.sigmoid(gt_ref[...].astype(F32))
        merged = term if merged is None else merged + term
    y = jnp.dot(merged.astype(BF16), wo_ref[...], preferred_element_type=F32)
    xn = x_ref[...] + g1_ref[0] * y
    xo_ref[...] = xn
    hn = xn * lax.rsqrt(jnp.mean(xn * xn, axis=-1, keepdims=True) + EPS)
    h2 = (hn * n2_ref[...]) * (1.0 + sc2_ref[0]) + sh2_ref[0]
    h2_ref[...] = h2
    lg_ref[...] = jnp.dot(h2, wr_ref[...], precision=lax.Precision.HIGHEST,
                          preferred_element_type=F32) + br_ref[...]


def _merge(o_att, o_hy, o_cf, proj2, gate_col0, wb_bf, wo_bf, x2, mod3, norm2_g, wr_pad, br_pad, seq):
    t, d = x2.shape
    bw = o_att.shape[1]
    tm = _tile(seq, 512)
    per_b = seq // tm
    gb = gate_col0 // d
    epad = wr_pad.shape[1]
    row = lambda w: pl.BlockSpec((tm, w), lambda i: (i, 0))
    modc = lambda c: pl.BlockSpec((1, 1, d), lambda i: (i // per_b, 0, c))
    return pl.pallas_call(
        _merge_kernel,
        grid=(t // tm,),
        in_specs=[row(bw), row(bw), row(bw),
                  pl.BlockSpec((tm, d), lambda i: (i, gb)),
                  pl.BlockSpec((tm, d), lambda i: (i, gb + 1)),
                  pl.BlockSpec((tm, d), lambda i: (i, gb + 2)),
                  pl.BlockSpec((3, bw, d), lambda i: (0, 0, 0)),
                  pl.BlockSpec((d, d), lambda i: (0, 0)),
                  row(d),
                  modc(2),
                  pl.BlockSpec((1, d), lambda i: (0, 0)),
                  modc(3), modc(4),
                  pl.BlockSpec((d, epad), lambda i: (0, 0)),
                  pl.BlockSpec((1, epad), lambda i: (0, 0))],
        out_specs=[row(d), row(d), row(epad)],
        out_shape=[jax.ShapeDtypeStruct((t, d), F32), jax.ShapeDtypeStruct((t, d), F32),
                   jax.ShapeDtypeStruct((t, epad), F32)],
        compiler_params=_cparams("arbitrary"),
        name="merge",
    )(o_att, o_hy, o_cf, proj2, proj2, proj2, wb_bf, wo_bf, x2, mod3, norm2_g.reshape(1, d),
      mod3, mod3, wr_pad, br_pad)


SEG_ALIGN = 8
SEG_CHUNK = 64


def _router_kernel(lg_ref, tri_ref, low_ref, w_ref, q_ref, tab_ref, cnt_ref, carry_scr, *, n_exp):
    @pl.when(pl.program_id(0) == 0)
    def _():
        carry_scr[...] = jnp.zeros_like(carry_scr)

    lt = lg_ref[...].T[:n_exp]
    eio = lax.broadcasted_iota(I32, lt.shape, 0)
    cur = lt
    vals, idxs = [], []
    for _ in range(TOP_K):
        m = jnp.max(cur, axis=0, keepdims=True)
        ik = jnp.min(jnp.where(cur == m, eio, n_exp), axis=0, keepdims=True)
        vals.append(m)
        idxs.append(ik)
        cur = jnp.where(eio == ik, -jnp.inf, cur)
    ex = [jnp.exp(v - vals[0]) for v in vals]
    den = ex[0] + ex[1] + ex[2] + ex[3]
    onehot = jnp.zeros(lt.shape, F32)
    for ik in idxs:
        onehot = onehot + jnp.where(eio == ik, 1.0, 0.0)
    incl = jnp.dot(onehot.astype(BF16), tri_ref[...], preferred_element_type=F32)
    count = incl[:, incl.shape[1] - 1:]
    padded = jnp.floor((count + (SEG_ALIGN - 1)) * (1.0 / SEG_ALIGN)) * SEG_ALIGN
    padded_b = jnp.broadcast_to(padded, carry_scr.shape)
    seg_start = jnp.dot(low_ref[...], padded_b, precision=lax.Precision.HIGHEST,
                        preferred_element_type=F32)
    local = incl - onehot + seg_start[:, 0:1]
    for k in range(TOP_K):
        w_ref[k:k + 1, :] = ex[k] / den
        q_ref[k:k + 1, :] = jnp.sum(jnp.where(eio == idxs[k], local, 0.0), axis=0, keepdims=True).astype(I32)
    carry = carry_scr[...]
    tab_ref[0, 0] = seg_start
    tab_ref[0, 1] = carry
    tab_ref[0, 2] = padded_b
    carry_scr[...] = carry + padded_b
    cnt_ref[...] = carry + padded_b


def _router(logits_pad, n_exp, tt):
    t, epad = logits_pad.shape
    n_tt = t // tt
    tri = (jnp.arange(tt)[:, None] <= jnp.arange(tt)[None, :]).astype(BF16)
    low = (jnp.arange(n_exp)[None, :] < jnp.arange(n_exp)[:, None]).astype(F32)
    return pl.pallas_call(
        functools.partial(_router_kernel, n_exp=n_exp),
        grid=(n_tt,),
        in_specs=[pl.BlockSpec((tt, epad), lambda i: (i, 0)),
                  pl.BlockSpec((tt, tt), lambda i: (0, 0)),
                  pl.BlockSpec((n_exp, n_exp), lambda i: (0, 0))],
        out_specs=[pl.BlockSpec((TOP_K, tt), lambda i: (0, i)),
                   pl.BlockSpec((TOP_K, tt), lambda i: (0, i)),
                   pl.BlockSpec((1, 3, n_exp, LANES), lambda i: (i, 0, 0, 0)),
                   pl.BlockSpec((n_exp, LANES), lambda i: (0, 0))],
        out_shape=[jax.ShapeDtypeStruct((TOP_K, t), F32), jax.ShapeDtypeStruct((TOP_K, t), I32),
                   jax.ShapeDtypeStruct((n_tt, 3, n_exp, LANES), F32),
                   jax.ShapeDtypeStruct((n_exp, LANES), F32)],
        scratch_shapes=[pltpu.VMEM((n_exp, LANES), F32)],
        compiler_params=_cparams("arbitrary"),
        name="router",
    )(logits_pad, tri, low)


def _row_window(ref, base, off, size):
    return ref.at[pl.ds(pl.multiple_of(base + off, SEG_ALIGN), size)]


def _seg_copies(src, src_base, dst, dst_base, length, sem, act):
    n_full = length // SEG_CHUNK

    def full(c, carry):
        off = c * SEG_CHUNK
        act(pltpu.make_async_copy(_row_window(src, src_base, off, SEG_CHUNK),
                                  _row_window(dst, dst_base, off, SEG_CHUNK), sem))
        return carry

    lax.fori_loop(0, n_full, full, 0)
    off = n_full * SEG_CHUNK
    size = SEG_CHUNK // 2
    while size >= SEG_ALIGN:
        bit = (length & size) != 0

        @pl.when(bit)
        def _(off=off, size=size):
            act(pltpu.make_async_copy(_row_window(src, src_base, off, size),
                                      _row_window(dst, dst_base, off, size), sem))

        off = off + jnp.where(bit, size, 0)
        size //= 2


def _start(copy):
    copy.start()


def _wait(copy):
    copy.wait()


def _sort_chunks(rows_total):
    n = 3 if rows_total % (3 * LANES) == 0 else 1
    return n, rows_total // n


def _dispatch_kernel(tab_s, tab_d, tab_m, tail_d, tail_z, n_valid, h_ref, q_ref, xs_hbm, sbuf, zbuf, sem, zsem,
                     *, n_exp):
    i = pl.program_id(0)
    n = pl.num_programs(0)
    slot = i % 2
    tt = h_ref.shape[0]
    rows_total = sbuf.shape[1]
    zrows = zbuf.shape[0]

    def tile_copies(j, s, act):
        def per_expert(e, carry):
            k = j * n_exp + e
            _seg_copies(sbuf.at[s], tab_s[k], xs_hbm, tab_d[k], tab_m[k], sem.at[s], act)
            return carry
        lax.fori_loop(0, n_exp, per_expert, 0)

    def tail_copies(act):
        def per_expert(e, carry):
            off = tail_d[e]
            size = zbuf.shape[0]
            while size >= SEG_ALIGN:
                bit = (tail_z[e] & size) != 0

                @pl.when(bit)
                def _(off=off, size=size):
                    act(pltpu.make_async_copy(zbuf.at[pl.ds(0, size)], _row_window(xs_hbm, off, 0, size),
                                              zsem.at[0]))

                off = off + jnp.where(bit, size, 0)
                size //= 2
            return carry
        lax.fori_loop(0, n_exp, per_expert, 0)

        def per_unused_tile(tile, carry):
            for half in range(2):
                act(pltpu.make_async_copy(zbuf, _row_window(xs_hbm, tile * (2 * zrows), half * zrows, zrows),
                                          zsem.at[0]))
            return carry
        lax.fori_loop(n_valid[0], xs_hbm.shape[0] // (2 * zrows), per_unused_tile, 0)

    @pl.when(i == 0)
    def _():
        zbuf[...] = jnp.zeros_like(zbuf)
        tail_copies(_start)

    @pl.when(i >= 2)
    def _():
        tile_copies(i - 2, slot, _wait)

    hb = h_ref[...].astype(BF16)
    n_chunks, rc = _sort_chunks(rows_total)
    for c in range(n_chunks):
        row = lax.broadcasted_iota(I32, (rc, tt), 0) + c * rc
        pi = jnp.zeros((rc, tt), F32)
        for k in range(TOP_K):
            pi = pi + jnp.where(row == q_ref[k:k + 1, :], 1.0, 0.0)
        sbuf[slot, c * rc:(c + 1) * rc] = jnp.dot(pi.astype(BF16), hb, preferred_element_type=F32)
    tile_copies(i, slot, _start)

    @pl.when(i == n - 1)
    def _():
        @pl.when(i >= 1)
        def _():
            tile_copies(i - 1, 1 - slot, _wait)
        tile_copies(i, slot, _wait)
        tail_copies(_wait)


def _dispatch(h2, q, tabs, tails, n_valid, n_rows, tt, n_exp, zrows):
    t, d = h2.shape
    rows_total = TOP_K * tt + SEG_ALIGN * n_exp
    grid_spec = pltpu.PrefetchScalarGridSpec(
        num_scalar_prefetch=6,
        grid=(t // tt,),
        in_specs=[pl.BlockSpec((tt, d), lambda i, *_: (i, 0)),
                  pl.BlockSpec((TOP_K, tt), lambda i, *_: (0, i))],
        out_specs=pl.BlockSpec(memory_space=pl.ANY),
        scratch_shapes=[pltpu.VMEM((2, rows_total, d), F32), pltpu.VMEM((zrows, d), F32),
                        pltpu.SemaphoreType.DMA((2,)), pltpu.SemaphoreType.DMA((1,))],
    )
    return pl.pallas_call(
        functools.partial(_dispatch_kernel, n_exp=n_exp),
        grid_spec=grid_spec,
        out_shape=jax.ShapeDtypeStruct((n_rows, d), F32),
        compiler_params=_cparams("arbitrary"),
        name="dispatch",
    )(*tabs, *tails, n_valid, h2, q)


def _expert_kernel(te_ref, nv_ref, x_ref, wgu_ref, bgu_ref, wdn_ref, bdn_ref, o_ref):
    del te_ref
    i = pl.program_id(0)
    ff = wdn_ref.shape[1]

    @pl.when(i < nv_ref[0])
    def _():
        xb = x_ref[...].astype(BF16)
        gu = jnp.dot(xb, wgu_ref[0], preferred_element_type=F32) + bgu_ref[0]
        g = jnp.minimum(gu[:, :ff], SWIGLU_LIMIT)
        u = jnp.clip(gu[:, ff:], -SWIGLU_LIMIT, SWIGLU_LIMIT)
        hd = (u + 1.0) * (g * jax.nn.sigmoid(SWIGLU_ALPHA * g))
        y = jnp.dot(hd.astype(BF16), wdn_ref[0], preferred_element_type=F32) + bdn_ref[0]
        o_ref[...] = y.astype(o_ref.dtype)

    @pl.when(i >= nv_ref[0])
    def _():
        o_ref[...] = jnp.zeros_like(o_ref)


def _experts(xs, tile_e, n_valid, wgu_bf, bgu, wdn_bf, bdn, tm):
    n_rows, d = xs.shape
    n_tiles = n_rows // tm
    n_exp, _, ff2 = wgu_bf.shape
    ff = ff2 // 2
    grid_spec = pltpu.PrefetchScalarGridSpec(
        num_scalar_prefetch=2,
        grid=(n_tiles,),
        in_specs=[pl.BlockSpec((tm, d), lambda i, te, nv: (jnp.minimum(i, nv[0] - 1), 0)),
                  pl.BlockSpec((1, d, ff2), lambda i, te, nv: (te[i], 0, 0)),
                  pl.BlockSpec((1, 1, ff2), lambda i, te, nv: (te[i], 0, 0)),
                  pl.BlockSpec((1, ff, d), lambda i, te, nv: (te[i], 0, 0)),
                  pl.BlockSpec((1, 1, d), lambda i, te, nv: (te[i], 0, 0))],
        out_specs=pl.BlockSpec((tm, d), lambda i, te, nv: (i, 0)),
    )
    return pl.pallas_call(
        _expert_kernel,
        grid_spec=grid_spec,
        out_shape=jax.ShapeDtypeStruct((n_rows, d), F32),
        compiler_params=_cparams("arbitrary"),
        name="experts",
    )(tile_e, n_valid, xs, wgu_bf, bgu.reshape(n_exp, 1, ff2), wdn_bf, bdn.reshape(n_exp, 1, d))


def _combine_kernel(tab_s, tab_d, tab_m, ys_hbm, q_ref, w_ref, x_ref, g2_ref, o_ref, cbuf, sem, *, n_exp):
    i = pl.program_id(0)
    n = pl.num_programs(0)
    slot = i % 2
    tt = x_ref.shape[0]
    rows_total = cbuf.shape[1]

    def tile_copies(j, s, act):
        def per_expert(e, carry):
            k = j * n_exp + e
            _seg_copies(ys_hbm, tab_d[k], cbuf.at[s], tab_s[k], tab_m[k], sem.at[s], act)
            return carry
        lax.fori_loop(0, n_exp, per_expert, 0)

    @pl.when(i == 0)
    def _():
        cbuf[...] = jnp.zeros_like(cbuf)
        tile_copies(0, 0, _start)

    @pl.when(i + 1 < n)
    def _():
        tile_copies(i + 1, 1 - slot, _start)

    tile_copies(i, slot, _wait)
    q = q_ref[...]
    w = w_ref[...]
    n_chunks, rc = _sort_chunks(rows_total)
    acc = jnp.zeros(x_ref.shape, F32)
    for c in range(n_chunks):
        col = lax.broadcasted_iota(I32, (tt, rc), 1) + c * rc
        pw = jnp.zeros((tt, rc), F32)
        for k in range(TOP_K):
            pw = pw + jnp.where(col == q[:, k:k + 1], w[:, k:k + 1], 0.0)
        acc = acc + jnp.dot(pw.astype(BF16), cbuf[slot, c * rc:(c + 1) * rc].astype(BF16),
                            preferred_element_type=F32)
    o_ref[...] = x_ref[...] + g2_ref[0] * acc


def _combine(ys, q_t, w_t, tabs, x2, mod3, seq, tt, n_exp):
    t, d = x2.shape
    per_b = seq // tt
    rows_total = TOP_K * tt + SEG_ALIGN * n_exp
    grid_spec = pltpu.PrefetchScalarGridSpec(
        num_scalar_prefetch=3,
        grid=(t // tt,),
        in_specs=[pl.BlockSpec(memory_space=pl.ANY),
                  pl.BlockSpec((tt, TOP_K), lambda i, *_: (i, 0)),
                  pl.BlockSpec((tt, TOP_K), lambda i, *_: (i, 0)),
                  pl.BlockSpec((tt, d), lambda i, *_: (i, 0)),
                  pl.BlockSpec((1, 1, d), lambda i, *_: (i // per_b, 0, 5))],
        out_specs=pl.BlockSpec((tt, d), lambda i, *_: (i, 0)),
        scratch_shapes=[pltpu.VMEM((2, rows_total, d), F32), pltpu.SemaphoreType.DMA((2,))],
    )
    return pl.pallas_call(
        functools.partial(_combine_kernel, n_exp=n_exp),
        grid_spec=grid_spec,
        out_shape=jax.ShapeDtypeStruct((t, d), F32),
        compiler_params=_cparams("arbitrary"),
        name="combine",
    )(*tabs, ys, q_t, w_t, x2, mod3)


def _moe(h2, logits_pad, x2, mod3, wgu_bf, bgu, wdn_bf, bdn, seq):
    t, d = x2.shape
    n_exp = wgu_bf.shape[0]
    tt = _tile(seq, 512)
    tm = _tile(t * TOP_K // n_exp, 512)
    n_tt = t // tt
    w, q, tab, cnt = _router(logits_pad, n_exp, tt)
    tab = tab[:, :, :, 0].astype(I32)
    used = cnt[:, 0].astype(I32)
    group = ((used + tm - 1) // tm) * tm
    gend = jnp.cumsum(group)
    goff = gend - group
    max_rows = TOP_K * t + n_tt * n_exp * (SEG_ALIGN - 1)
    n_tiles = -(-max_rows // tm) + n_exp
    tile_start = jnp.arange(n_tiles, dtype=I32) * tm
    tile_e = jnp.minimum(jnp.sum((tile_start[:, None] >= gend[None, :]).astype(I32), axis=1), n_exp - 1)
    n_valid = (gend[-1:] // tm).astype(I32)
    tabs = (tab[:, 0].reshape(-1), (tab[:, 1] + goff[None, :]).reshape(-1), tab[:, 2].reshape(-1))
    tails = (goff + used, group - used)
    xs = _dispatch(h2, q, tabs, tails, n_valid, n_tiles * tm, tt, n_exp, tm // 2)
    ys = _experts(xs, tile_e, n_valid, wgu_bf, bgu, wdn_bf, bdn, tm)
    return _combine(ys, q.T, w.T, tabs, x2, mod3, seq, tt, n_exp)


def kernel(x, c, norm1_g, norm2_g, w_ada, b_ada, w_in, b_in, q_norm_g, k_norm_g, lam_q1, lam_k1, lam_q2, lam_k2, subln_g, hy_conv_w, hy_conv_b, hf_w1, hf_b1, hf_w2, hf_b2, hf_w3, hf_b3, hf_freq, hf_w4, hy_bias, cf_dw_w, cf_dw_b, cf_ln_g, cf_ln_b, w_branch, w_out, w_router, b_router, w_gu, b_gu, w_down, b_down):
    bsz, seq, d = x.shape
    depth = w_ada.shape[0]
    t = bsz * seq
    bw = d // 2
    att_w = ATT_HEADS * 2 * HEAD_DIM
    hy_col0 = 3 * att_w
    cf_col0 = hy_col0 + 3 * bw
    gate_col0 = cf_col0 + 2 * bw
    n_exp = w_router.shape[2]
    assert att_w == bw and bsz % 2 == 0 and seq % LANES == 0

    hp = dict(hf_w1=hf_w1, hf_b1=hf_b1, hf_w2=hf_w2, hf_b2=hf_b2, hf_w3=hf_w3, hf_b3=hf_b3,
              hf_freq=hf_freq, hf_w4=hf_w4, hy_conv_w=hy_conv_w, hy_conv_b=hy_conv_b, hy_bias=hy_bias)

    pos = jnp.arange(seq, dtype=F32)
    inv = 1.0 / (ROPE_THETA ** (jnp.arange(0, HEAD_DIM, 2, dtype=F32) / HEAD_DIM))
    ang = pos[:, None] * inv[None, :]
    ang = jnp.concatenate([ang, ang], axis=-1)
    sign = jnp.where(jnp.arange(HEAD_DIM) < HEAD_DIM // 2, -1.0, 1.0).astype(F32)
    reps = att_w // HEAD_DIM
    cos_t = jnp.tile(jnp.cos(ang), (1, reps))
    sin_t = jnp.tile(jnp.sin(ang) * sign[None, :], (1, reps))
    lane = jnp.arange(att_w)
    gsum = (lane[:, None] // HEAD_DIM == lane[None, :] // HEAD_DIM).astype(BF16)
    dft = _dft_tables(seq)
    tabs = _hy_tables(seq, bw)
    epad = LANES

    mod_all = _ada_all(c, w_ada, b_ada)
    x2 = x.reshape(t, d)
    for l in range(depth):
        mod3 = mod_all[l].reshape(bsz, 1, 6 * d)
        lam_init = 0.8 - 0.6 * math.exp(-0.3 * l)

        proj2 = _in_proj(x2, mod3, norm1_g[l], w_in[l].astype(BF16), b_in[l], seq)
        proj3 = proj2.reshape(bsz, seq, -1)

        qr, kt = _qk_prep(proj3, cos_t, sin_t, jnp.tile(q_norm_g[l], reps).reshape(1, att_w),
                          jnp.tile(k_norm_g[l], reps).reshape(1, att_w), gsum)
        lamv = jnp.stack([lam_q1[l], lam_k1[l], lam_q2[l], lam_k2[l]])
        o_att = _attention(qr, kt, proj3, lamv, subln_g[l], lam_init)

        o_hy = _hyena(proj3, hp, l, dft, tabs, hy_col0, bw)
        o_cf = _conformer(proj3, cf_dw_w[l], cf_dw_b[l], cf_ln_g[l], cf_ln_b[l], cf_col0, bw)

        wr_pad = jnp.pad(w_router[l], ((0, 0), (0, epad - n_exp)))
        br_pad = jnp.pad(b_router[l], (0, epad - n_exp)).reshape(1, epad)
        x2, h2, logits = _merge(o_att.reshape(t, bw), o_hy.reshape(t, bw), o_cf.reshape(t, bw), proj2,
                                gate_col0, w_branch[l].astype(BF16), w_out[l].astype(BF16), x2, mod3,
                                norm2_g[l], wr_pad, br_pad, seq)
        x2 = _moe(h2, logits, x2, mod3, w_gu[l].astype(BF16), b_gu[l], w_down[l].astype(BF16),
                  b_down[l], seq)
    return x2.reshape(bsz, seq, d)
```

```python
import functools
import math

import jax
import jax.numpy as jnp
from jax import lax
from jax.experimental import pallas as pl
from jax.experimental.pallas import tpu as pltpu

F32 = jnp.float32
BF16 = jnp.bfloat16
I32 = jnp.int32

LANES = 128
SUBLANES = 8
VMEM_LIMIT_BYTES = 56 * 1024 * 1024

ATT_HEADS = 4
HEAD_DIM = 64
V_DIM = 2 * HEAD_DIM
ROPE_THETA = 10000.0
HY_BANDS = 16
HY_FAST_DECAY = 0.3
HY_SLOW_DECAY = 1.5
HY_TARGET = 1e-2
HY_SHIFT = 0.05
CF_KERNEL = 31
CF_HALO = 16
TOP_K = 4
SWIGLU_LIMIT = 7.0
SWIGLU_ALPHA = 1.702
EPS = 1e-6
LOG2E = 1.4426950408889634


def _cparams(*sem):
    return pltpu.CompilerParams(dimension_semantics=sem, vmem_limit_bytes=VMEM_LIMIT_BYTES)


def _tile(n, want):
    t = min(n, want)
    while n % t:
        t -= 1
    return t


def _ada_kernel(c_ref, w_ref, b_ref, o_ref):
    c = c_ref[...]
    cond = c * jax.nn.sigmoid(c)
    o_ref[0] = jnp.dot(cond.astype(BF16), w_ref[0].astype(BF16),
                       preferred_element_type=F32) + b_ref[0]


def _ada_all(c, w_ada, b_ada):
    depth, d, n = w_ada.shape
    bsz = c.shape[0]
    tn = _tile(n, 1536)
    return pl.pallas_call(
        _ada_kernel,
        grid=(depth, n // tn),
        in_specs=[pl.BlockSpec((bsz, d), lambda l, j: (0, 0)),
                  pl.BlockSpec((1, d, tn), lambda l, j: (l, 0, j)),
                  pl.BlockSpec((1, 1, tn), lambda l, j: (l, 0, j))],
        out_specs=pl.BlockSpec((1, bsz, tn), lambda l, j: (l, 0, j)),
        out_shape=jax.ShapeDtypeStruct((depth, bsz, n), F32),
        compiler_params=_cparams("arbitrary", "arbitrary"),
        name="ada_mod",
    )(c, w_ada, b_ada.reshape(depth, 1, n))


def _proj_kernel(x_ref, sh_ref, sc_ref, g_ref, w_ref, b_ref, o_ref, h_scr):
    @pl.when(pl.program_id(1) == 0)
    def _():
        x = x_ref[...]
        y = x * lax.rsqrt(jnp.mean(x * x, axis=-1, keepdims=True) + EPS)
        h = (y * g_ref[...]) * (1.0 + sc_ref[0]) + sh_ref[0]
        h_scr[...] = h.astype(BF16)

    acc = jnp.dot(h_scr[...], w_ref[...], preferred_element_type=F32)
    o_ref[...] = (acc + b_ref[...]).astype(o_ref.dtype)


def _in_proj(x2, mod3, norm_g, w_bf, b, seq):
    t, d = x2.shape
    n = w_bf.shape[1]
    tm = _tile(seq, 1024)
    tn = _tile(n, 1792)
    per_b = seq // tm
    return pl.pallas_call(
        _proj_kernel,
        grid=(t // tm, n // tn),
        in_specs=[pl.BlockSpec((tm, d), lambda i, j: (i, 0)),
                  pl.BlockSpec((1, 1, d), lambda i, j: (i // per_b, 0, 0)),
                  pl.BlockSpec((1, 1, d), lambda i, j: (i // per_b, 0, 1)),
                  pl.BlockSpec((1, d), lambda i, j: (0, 0)),
                  pl.BlockSpec((d, tn), lambda i, j: (0, j)),
                  pl.BlockSpec((1, tn), lambda i, j: (0, j))],
        out_specs=pl.BlockSpec((tm, tn), lambda i, j: (i, j)),
        out_shape=jax.ShapeDtypeStruct((t, n), BF16),
        scratch_shapes=[pltpu.VMEM((tm, d), BF16)],
        compiler_params=_cparams("arbitrary", "arbitrary"),
        name="in_proj",
    )(x2, mod3, mod3, norm_g.reshape(1, d), w_bf, b.reshape(1, n))


def _qkprep_kernel(q_ref, k_ref, cos_ref, sin_ref, gq_ref, gk_ref, gs_ref, qo_ref, kto_ref, *, q_scale):
    width = q_ref.shape[-1]
    lane = lax.broadcasted_iota(I32, q_ref.shape[1:], 1)
    first_half = (lane % HEAD_DIM) < (HEAD_DIM // 2)

    def prep(t, g):
        ssq = jnp.dot((t * t).astype(BF16), gs_ref[...], preferred_element_type=F32)
        y = t * lax.rsqrt(ssq * (1.0 / HEAD_DIM) + EPS) * g
        rot = jnp.where(first_half,
                        pltpu.roll(y, width - HEAD_DIM // 2, 1),
                        pltpu.roll(y, HEAD_DIM // 2, 1))
        return y * cos_ref[...] + rot * sin_ref[...]

    q = prep(q_ref[0].astype(F32), gq_ref[...])
    qo_ref[0] = (q * q_scale).astype(BF16)
    k = prep(k_ref[0].astype(F32), gk_ref[...])
    kto_ref[0] = k.T.astype(BF16)


def _qk_prep(proj3, cos_t, sin_t, gq, gk, gsum):
    bsz, seq, _ = proj3.shape
    width = ATT_HEADS * 2 * HEAD_DIM
    tr = _tile(seq, 512)
    q_scale = HEAD_DIM ** -0.5 * LOG2E
    return pl.pallas_call(
        functools.partial(_qkprep_kernel, q_scale=q_scale),
        grid=(bsz, seq // tr),
        in_specs=[pl.BlockSpec((1, tr, width), lambda b, i: (b, i, 0)),
                  pl.BlockSpec((1, tr, width), lambda b, i: (b, i, 1)),
                  pl.BlockSpec((tr, width), lambda b, i: (i, 0)),
                  pl.BlockSpec((tr, width), lambda b, i: (i, 0)),
                  pl.BlockSpec((1, width), lambda b, i: (0, 0)),
                  pl.BlockSpec((1, width), lambda b, i: (0, 0)),
                  pl.BlockSpec((width, width), lambda b, i: (0, 0))],
        out_specs=[pl.BlockSpec((1, tr, width), lambda b, i: (b, i, 0)),
                   pl.BlockSpec((1, width, tr), lambda b, i: (b, 0, i))],
        out_shape=[jax.ShapeDtypeStruct((bsz, seq, width), BF16),
                   jax.ShapeDtypeStruct((bsz, width, seq), BF16)],
        compiler_params=_cparams("arbitrary", "arbitrary"),
        name="qk_prep",
    )(proj3, proj3, cos_t, sin_t, gq, gk, gsum)


def _attn_kernel(q_ref, kt_ref, v_ref, lamv_ref, g_ref, o_ref, *, lam_init):
    lamv = lamv_ref[...]
    lam = (jnp.exp(jnp.sum(lamv[0:1] * lamv[1:2], axis=-1, keepdims=True))
           - jnp.exp(jnp.sum(lamv[2:3] * lamv[3:4], axis=-1, keepdims=True)) + lam_init)
    q = q_ref[0]
    seq = v_ref.shape[1]
    ones = jnp.ones((seq, V_DIM), BF16)
    for h in range(ATT_HEADS):
        v_ext = jnp.concatenate([v_ref[0, :, h * V_DIM:(h + 1) * V_DIM], ones], axis=1)
        outs = []
        for comp in range(2):
            lo = (2 * h + comp) * HEAD_DIM
            s = jnp.dot(q[:, lo:lo + HEAD_DIM], kt_ref[0, lo:lo + HEAD_DIM, :],
                        preferred_element_type=F32)
            p = jnp.exp2((s - jnp.max(s, axis=-1, keepdims=True)).astype(BF16))
            ov = jnp.dot(p, v_ext, preferred_element_type=F32)
            outs.append(ov[:, :V_DIM] * (1.0 / ov[:, V_DIM:V_DIM + 1]))
        o = outs[0] - lam * outs[1]
        o = o * lax.rsqrt(jnp.mean(o * o, axis=-1, keepdims=True) + EPS)
        o_ref[0, :, h * V_DIM:(h + 1) * V_DIM] = (o * g_ref[...] * (1.0 - lam_init)).astype(o_ref.dtype)


def _attention(qr, kt, proj3, lamv, subln_g, lam_init):
    bsz, seq, width = qr.shape
    tq = _tile(seq, 256)
    return pl.pallas_call(
        functools.partial(_attn_kernel, lam_init=lam_init),
        grid=(bsz, seq // tq),
        in_specs=[pl.BlockSpec((1, tq, width), lambda b, i: (b, i, 0)),
                  pl.BlockSpec((1, width, seq), lambda b, i: (b, 0, 0)),
                  pl.BlockSpec((1, seq, width), lambda b, i: (b, 0, 2)),
                  pl.BlockSpec((4, HEAD_DIM), lambda b, i: (0, 0)),
                  pl.BlockSpec((1, V_DIM), lambda b, i: (0, 0))],
        out_specs=pl.BlockSpec((1, tq, width), lambda b, i: (b, i, 0)),
        out_shape=jax.ShapeDtypeStruct((bsz, seq, width), BF16),
        compiler_params=_cparams("arbitrary", "arbitrary"),
        name="diff_attn",
    )(qr, kt, proj3, lamv, subln_g.reshape(1, V_DIM))


def _hyshort_kernel(u_ref, w_ref, b_ref, o_ref):
    u = u_ref[0].astype(F32)
    seq = u.shape[0]
    row = lax.broadcasted_iota(I32, u.shape, 0)
    prev = jnp.where(row == 0, 0.0, pltpu.roll(u, 1, 0))
    nxt = jnp.where(row == seq - 1, 0.0, pltpu.roll(u, seq - 1, 0))
    w = w_ref[...]
    y = w[0:1] * prev + w[1:2] * u + w[2:3] * nxt + b_ref[...]
    o_ref[0, 0] = y.astype(o_ref.dtype)


def _hy_short(proj3, conv_w, conv_b, col0, width):
    bsz, seq, _ = proj3.shape
    cb = LANES
    per = width // cb
    return pl.pallas_call(
        _hyshort_kernel,
        grid=(bsz, 3 * per),
        in_specs=[pl.BlockSpec((1, seq, cb), lambda b, j: (b, 0, col0 // cb + j)),
                  pl.BlockSpec((3, cb), lambda b, j: (0, j)),
                  pl.BlockSpec((1, cb), lambda b, j: (0, j))],
        out_specs=pl.BlockSpec((1, 1, seq, cb), lambda b, j: (j // per, b, 0, j % per)),
        out_shape=jax.ShapeDtypeStruct((3, bsz, seq, width), BF16),
        compiler_params=_cparams("arbitrary", "arbitrary"),
        name="hy_short",
    )(proj3, conv_w, conv_b.reshape(1, 3 * width))


def _filt_kernel(zz_ref, w1_ref, b1_ref, w2_ref, b2_ref, w3_ref, b3_ref, fr_ref,
                 w4f_ref, w4b_ref, wf_ref, wb_ref, o_ref):
    hi = lax.Precision.HIGHEST
    fr = fr_ref[...]
    h = jnp.sin(fr * (jnp.dot(zz_ref[...], w1_ref[...], precision=hi, preferred_element_type=F32) + b1_ref[...]))
    h = jnp.sin(fr * (jnp.dot(h, w2_ref[...], precision=hi, preferred_element_type=F32) + b2_ref[...]))
    h = jnp.sin(fr * (jnp.dot(h, w3_ref[...], precision=hi, preferred_element_type=F32) + b3_ref[...]))
    hb = h.astype(BF16)
    width = wf_ref.shape[-1]
    half = pl.program_id(0)
    first_tile = pl.program_id(1) == 0

    def emit(use_fwd, use_bwd):
        ff = jnp.dot(hb, w4f_ref[...], preferred_element_type=F32) if use_fwd else None
        fb = jnp.dot(hb, w4b_ref[...], preferred_element_type=F32) if use_bwd else None
        for n in range(2):
            sl = slice(n * width, (n + 1) * width)
            val = ff[:, sl] * wf_ref[...] if use_fwd else None
            if use_bwd:
                bwd = fb[:, sl] * wb_ref[...]
                val = bwd if val is None else val + bwd
            o_ref[0, n] = val.astype(o_ref.dtype)

    @pl.when(jnp.logical_and(half == 0, first_tile))
    def _():
        emit(True, True)

    @pl.when(jnp.logical_and(half == 0, jnp.logical_not(first_tile)))
    def _():
        emit(True, False)

    @pl.when(half == 1)
    def _():
        emit(False, True)


def _hy_kernels(tabs, w1p, b1, w2, b2, w3, b3, freq, w4f, w4b, seq, width):
    zz, wf, wb = tabs
    hid = w2.shape[0]
    tr = _tile(seq, 512)
    nt = seq // tr
    emb = zz.shape[1]
    full = lambda shape: pl.BlockSpec(shape, lambda hf, i: tuple(0 for _ in shape))
    return pl.pallas_call(
        _filt_kernel,
        grid=(2, nt),
        in_specs=[pl.BlockSpec((tr, emb), lambda hf, i: (hf * nt + i, 0)),
                  full((emb, hid)), full((1, hid)), full((hid, hid)), full((1, hid)),
                  full((hid, hid)), full((1, hid)), full((1, hid)),
                  full((hid, 2 * width)), full((hid, 2 * width)),
                  pl.BlockSpec((tr, width), lambda hf, i: (hf * nt + i, 0)),
                  pl.BlockSpec((tr, width), lambda hf, i: (hf * nt + i, 0))],
        out_specs=pl.BlockSpec((1, 2, tr, width), lambda hf, i: (hf, 0, i, 0)),
        out_shape=jax.ShapeDtypeStruct((2, 2, seq, width), BF16),
        compiler_params=_cparams("arbitrary", "arbitrary"),
        name="hy_filter",
    )(zz, w1p, b1.reshape(1, hid), w2, b2.reshape(1, hid), w3, b3.reshape(1, hid),
      freq.reshape(1, hid), w4f, w4b, wf, wb)


def _fft_a_kernel(m_ref, xa_ref, xb_ref, o_ref):
    d = jnp.concatenate([xa_ref[0].astype(BF16), xb_ref[0].astype(BF16)], axis=0)
    o_ref[0] = jnp.dot(m_ref[...], d, preferred_element_type=F32).astype(o_ref.dtype)


def _fft_a(m1, xa, xb, pairs, amap, bmap):
    nh, cols = xa.shape[1], xa.shape[2]
    rows = m1.shape[0]
    tc = _tile(cols, 8192)
    return pl.pallas_call(
        _fft_a_kernel,
        grid=(cols // tc, pairs),
        in_specs=[pl.BlockSpec(m1.shape, lambda j, p: (0, 0)),
                  pl.BlockSpec((1, nh, tc), lambda j, p: (amap(p), 0, j)),
                  pl.BlockSpec((1, nh, tc), lambda j, p: (bmap(p), 0, j))],
        out_specs=pl.BlockSpec((1, rows, tc), lambda j, p: (p, 0, j)),
        out_shape=jax.ShapeDtypeStruct((pairs, rows, cols), BF16),
        compiler_params=_cparams("arbitrary", "arbitrary"),
        name="fft_a",
    )(m1, xa, xb)


def _fft_b_kernel(are_ref, aim_ref, g_ref, gi_ref, kf_ref, vre_ref, vim_ref):
    kb = are_ref.shape[1]
    half = are_ref.shape[2]
    for q in range(kb):
        d = jnp.concatenate([are_ref[0, q], aim_ref[0, q]], axis=0)
        x = jnp.dot(g_ref[q], d, preferred_element_type=F32)
        xr, xi = x[:half], x[half:]
        kr, ki = kf_ref[q, :half], kf_ref[q, half:]
        p = jnp.concatenate([xr * kr - xi * ki, xr * ki + xi * kr], axis=0).astype(BF16)
        v = jnp.dot(gi_ref[q], p, preferred_element_type=F32)
        vre_ref[0, q] = v[:half].astype(vre_ref.dtype)
        vim_ref[0, q] = v[half:].astype(vim_ref.dtype)


def _fft_b(a4, g, gi, kf, n2):
    pairs, _, half, width = a4.shape
    kb = _tile(n2, 8)
    nk = n2 // kb
    return pl.pallas_call(
        _fft_b_kernel,
        grid=(nk, pairs),
        in_specs=[pl.BlockSpec((1, kb, half, width), lambda i, p: (p, i, 0, 0)),
                  pl.BlockSpec((1, kb, half, width), lambda i, p: (p, nk + i, 0, 0)),
                  pl.BlockSpec((kb, 2 * half, 2 * half), lambda i, p: (i, 0, 0)),
                  pl.BlockSpec((kb, 2 * half, 2 * half), lambda i, p: (i, 0, 0)),
                  pl.BlockSpec((kb, 2 * half, width), lambda i, p: (i, 0, 0))],
        out_specs=[pl.BlockSpec((1, kb, half, width), lambda i, p: (p, i, 0, 0)),
                   pl.BlockSpec((1, kb, half, width), lambda i, p: (p, i, 0, 0))],
        out_shape=[jax.ShapeDtypeStruct((pairs, n2, half, width), BF16),
                   jax.ShapeDtypeStruct((pairs, n2, half, width), BF16)],
        compiler_params=_cparams("arbitrary", "arbitrary"),
        name="fft_b",
    )(a4, a4, g, gi, kf)


def _fft_bf_kernel(are_ref, aim_ref, g_ref, kf_ref):
    kb = are_ref.shape[1]
    for q in range(kb):
        d = jnp.concatenate([are_ref[0, q], aim_ref[0, q]], axis=0)
        kf_ref[0, q] = jnp.dot(g_ref[q], d, preferred_element_type=F32)


def _fft_bf(a4, g, n2):
    pairs, _, half, width = a4.shape
    kb = _tile(n2, 8)
    nk = n2 // kb
    return pl.pallas_call(
        _fft_bf_kernel,
        grid=(nk, pairs),
        in_specs=[pl.BlockSpec((1, kb, half, width), lambda i, p: (p, i, 0, 0)),
                  pl.BlockSpec((1, kb, half, width), lambda i, p: (p, nk + i, 0, 0)),
                  pl.BlockSpec((kb, 2 * half, 2 * half), lambda i, p: (i, 0, 0))],
        out_specs=pl.BlockSpec((1, kb, 2 * half, width), lambda i, p: (p, i, 0, 0)),
        out_shape=jax.ShapeDtypeStruct((pairs, n2, 2 * half, width), F32),
        compiler_params=_cparams("arbitrary", "arbitrary"),
        name="fft_bf",
    )(a4, a4, g)


def _fft_c_kernel(m_ref, vre_ref, vim_ref, xg_ref, z_ref, db_ref, o_ref):
    d = jnp.concatenate([vre_ref[0], vim_ref[0]], axis=0)
    y = jnp.dot(m_ref[...], d, preferred_element_type=F32)
    nh = y.shape[0] // 2
    for s in range(2):
        conv = y[s * nh:(s + 1) * nh]
        z = z_ref[s].astype(F32)
        o_ref[s] = (xg_ref[s].astype(F32) * (conv + db_ref[...] * z)).astype(o_ref.dtype)


def _fft_c(m2, vre, vim, xg, z, dbias_t):
    bsz, nh, cols = z.shape
    pairs, n2, _ = vre.shape
    tc = _tile(cols, 8192)
    return pl.pallas_call(
        _fft_c_kernel,
        grid=(cols // tc, pairs),
        in_specs=[pl.BlockSpec(m2.shape, lambda j, p: (0, 0)),
                  pl.BlockSpec((1, n2, tc), lambda j, p: (p, 0, j)),
                  pl.BlockSpec((1, n2, tc), lambda j, p: (p, 0, j)),
                  pl.BlockSpec((2, nh, tc), lambda j, p: (p, 0, j)),
                  pl.BlockSpec((2, nh, tc), lambda j, p: (p, 0, j)),
                  pl.BlockSpec((1, tc), lambda j, p: (0, j))],
        out_specs=pl.BlockSpec((2, nh, tc), lambda j, p: (p, 0, j)),
        out_shape=jax.ShapeDtypeStruct((bsz, nh, cols), BF16),
        compiler_params=_cparams("arbitrary", "arbitrary"),
        name="fft_c",
    )(m2, vre, vim, xg, z, dbias_t)


def _stack_complex(zr, zi):
    return jnp.concatenate([jnp.concatenate([zr, -zi], axis=-1),
                            jnp.concatenate([zi, zr], axis=-1)], axis=-2)


def _dft_tables(seq):
    n = 2 * seq
    n2 = n // LANES
    nh = n2 // 2
    ar = lambda m: jnp.arange(m, dtype=I32)

    def cis(num, den, sign):
        ang = (2.0 * math.pi / den) * (num % den).astype(F32)
        return jnp.cos(ang), sign * jnp.sin(ang)

    fr, fi = cis(ar(n2)[:, None] * ar(n2)[None, :], n2, -1.0)
    m1_data = _stack_complex(fr[:, :nh], fi[:, :nh])
    m1_filt = jnp.concatenate([fr, fi], axis=0)
    m2 = _stack_complex(fr.T[:nh] / n, -fi.T[:nh] / n)
    k2 = ar(n2)[:, None, None]
    k1 = ar(LANES)[None, :, None]
    n1 = ar(LANES)[None, None, :]
    er, ei = cis(n1 * (n2 * k1 + k2), n, -1.0)
    g = _stack_complex(er, ei)
    gi = _stack_complex(jnp.swapaxes(er, 1, 2), -jnp.swapaxes(ei, 1, 2))
    return (m1_data.astype(BF16), m1_filt.astype(BF16), m2.astype(BF16), g.astype(BF16), gi.astype(BF16), n2)


def _hy_tables(seq, width):
    pos = jnp.arange(seq, dtype=F32)
    t = jnp.linspace(0.0, 1.0, seq, dtype=F32)[:, None]
    bands = jnp.linspace(1e-4, HY_BANDS - 1, HY_BANDS, dtype=F32)
    ang = (2.0 * math.pi / seq) * pos[:, None] * bands[None, :]
    z = jnp.concatenate([t, jnp.cos(ang), -jnp.sin(ang)], axis=-1)
    deltas = jnp.abs(jnp.linspace(math.log(HY_TARGET) / HY_FAST_DECAY,
                                  math.log(HY_TARGET) / HY_SLOW_DECAY, width, dtype=F32))
    window = jnp.exp(-t * deltas[None, :]) + HY_SHIFT
    mirror = jnp.concatenate([jnp.zeros((1,), I32), jnp.arange(seq - 1, 0, -1, dtype=I32)])
    zz = jnp.concatenate([z, z[mirror]], axis=0)
    emb = z.shape[1]
    emb_pad = ((emb + 63) // 64) * 64
    zz = jnp.pad(zz, ((0, 0), (0, emb_pad - emb)))
    zero = jnp.zeros_like(window)
    first = (jnp.arange(seq) == 0)[:, None]
    wf = jnp.concatenate([window, zero], axis=0)
    wb = jnp.concatenate([jnp.where(first, window, 0.0),
                          jnp.where(first, 0.0, window[mirror])], axis=0)
    return zz, wf, wb


def _hyena(proj3, p, l, dft, tabs, col0, width):
    bsz, seq, _ = proj3.shape
    m1_data, m1_filt, m2, g, gi, n2 = dft
    nh = n2 // 2
    cols = LANES * width
    pairs = bsz // 2
    emb_pad = tabs[0].shape[1]
    w1p = jnp.pad(p['hf_w1'][l], ((0, emb_pad - p['hf_w1'].shape[1]), (0, 0)))
    hid = p['hf_w4'].shape[1]
    w4 = p['hf_w4'][l].reshape(hid, 2, 2, width)
    w4f = w4[:, :, 0, :].reshape(hid, 2 * width).astype(BF16)
    w4b = w4[:, :, 1, :].reshape(hid, 2 * width).astype(BF16)
    kern = _hy_kernels(tabs, w1p, p['hf_b1'][l], p['hf_w2'][l], p['hf_b2'][l], p['hf_w3'][l], p['hf_b3'][l],
                       p['hf_freq'][l], w4f, w4b, seq, width)
    ka = _fft_a(m1_filt, kern[0].reshape(2, nh, cols), kern[1].reshape(2, nh, cols), 2,
                lambda q: q, lambda q: q)
    kf = _fft_bf(ka.reshape(2, 2 * n2, LANES, width), g, n2)

    vxx = _hy_short(proj3, p['hy_conv_w'][l], p['hy_conv_b'][l], col0, width)
    z = vxx[0].reshape(bsz, nh, cols)
    for order in range(2):
        a = _fft_a(m1_data, z, z, pairs, lambda q: 2 * q, lambda q: 2 * q + 1)
        vre, vim = _fft_b(a.reshape(pairs, 2 * n2, LANES, width), g, gi, kf[order], n2)
        dbias_t = jnp.tile(p['hy_bias'][l, order], LANES).reshape(1, cols)
        z = _fft_c(m2, vre.reshape(pairs, n2, cols), vim.reshape(pairs, n2, cols),
                   vxx[1 + order].reshape(bsz, nh, cols), z, dbias_t)
    return z.reshape(bsz, seq, width)


CF_ROWS = 64


def _conf_kernel(ap_ref, a_ref, an_ref, gp_ref, g_ref, gn_ref, w_ref, b_ref, lg_ref, lb_ref,
                 o_ref, u_scr, acc_scr):
    i = pl.program_id(1)
    last = pl.num_programs(1) - 1
    tl = a_ref.shape[1]
    width = a_ref.shape[2]

    def glu(a, g):
        return a.astype(F32) * jax.nn.sigmoid(g.astype(F32))

    u_scr[0:CF_HALO] = jnp.where(i > 0, glu(ap_ref[0], gp_ref[0]), 0.0)
    u_scr[CF_HALO:CF_HALO + tl] = glu(a_ref[0], g_ref[0])
    u_scr[CF_HALO + tl:2 * CF_HALO + tl] = jnp.where(i < last, glu(an_ref[0], gn_ref[0]), 0.0)

    shift = CF_HALO - CF_KERNEL // 2

    def rows(r, carry):
        r0 = pl.multiple_of(r * CF_ROWS, CF_ROWS)
        for c in range(width // LANES):
            cs = slice(c * LANES, (c + 1) * LANES)
            win = u_scr[pl.ds(r0, CF_ROWS + 2 * CF_HALO), cs]
            n_win = win.shape[0]
            rolled = [win] + [pltpu.roll(win, n_win - b, 0) for b in range(1, SUBLANES)]
            acc = jnp.zeros((CF_ROWS, LANES), F32)
            for j in range(CF_KERNEL):
                a, b = divmod(shift + j, SUBLANES)
                acc = acc + w_ref[j:j + 1, cs] * rolled[b][a * SUBLANES:a * SUBLANES + CF_ROWS]
            acc_scr[pl.ds(r0, CF_ROWS), cs] = acc
        return carry

    lax.fori_loop(0, tl // CF_ROWS, rows, 0)
    y = acc_scr[...] + b_ref[...]
    mu = jnp.mean(y, axis=-1, keepdims=True)
    yc = y - mu
    var = jnp.mean(yc * yc, axis=-1, keepdims=True)
    yn = yc * lax.rsqrt(var + EPS) * lg_ref[...] + lb_ref[...]
    o_ref[0] = (yn * jax.nn.sigmoid(yn)).astype(o_ref.dtype)


def _conformer(proj3, dw_w, dw_b, ln_g, ln_b, col0, width):
    bsz, seq, _ = proj3.shape
    tl = _tile(seq, 512)
    a_blk = col0 // width
    g_blk = a_blk + 1
    hpt = tl // CF_HALO
    nhalo = seq // CF_HALO

    def cur(blk):
        return pl.BlockSpec((1, tl, width), lambda b, i: (b, i, blk))

    def prev(blk):
        return pl.BlockSpec((1, CF_HALO, width), lambda b, i: (b, jnp.maximum(i * hpt - 1, 0), blk))

    def nxt(blk):
        return pl.BlockSpec((1, CF_HALO, width), lambda b, i: (b, jnp.minimum((i + 1) * hpt, nhalo - 1), blk))

    vec = lambda rows: pl.BlockSpec((rows, width), lambda b, i: (0, 0))
    return pl.pallas_call(
        _conf_kernel,
        grid=(bsz, seq // tl),
        in_specs=[prev(a_blk), cur(a_blk), nxt(a_blk), prev(g_blk), cur(g_blk), nxt(g_blk),
                  vec(CF_KERNEL), vec(1), vec(1), vec(1)],
        out_specs=pl.BlockSpec((1, tl, width), lambda b, i: (b, i, 0)),
        out_shape=jax.ShapeDtypeStruct((bsz, seq, width), BF16),
        scratch_shapes=[pltpu.VMEM((tl + 2 * CF_HALO, width), F32), pltpu.VMEM((tl, width), F32)],
        compiler_params=_cparams("arbitrary", "arbitrary"),
        name="conformer",
    )(proj3, proj3, proj3, proj3, proj3, proj3, dw_w, dw_b.reshape(1, width),
      ln_g.reshape(1, width), ln_b.reshape(1, width))


def _merge_kernel(oa_ref, oh_ref, oc_ref, ga_ref, gh_ref, gc_ref, wb_ref, wo_ref, x_ref, g1_ref,
                  n2_ref, sh2_ref, sc2_ref, wr_ref, br_ref, xo_ref, h2_ref, lg_ref):
    merged = None
    for n, (o_ref, gt_ref) in enumerate(((oa_ref, ga_ref), (oh_ref, gh_ref), (oc_ref, gc_ref))):
        br = jnp.dot(o_ref[...], wb_ref[n], preferred_element_type=F32)
        term = br * jax.nn.sigmoid(gt_ref[...].astype(F32))
        merged = term if merged is None else merged + term
    y = jnp.dot(merged.astype(BF16), wo_ref[...], preferred_element_type=F32)
    xn = x_ref[...] + g1_ref[0] * y
    xo_ref[...] = xn
    hn = xn * lax.rsqrt(jnp.mean(xn * xn, axis=-1, keepdims=True) + EPS)
    h2 = (hn * n2_ref[...]) * (1.0 + sc2_ref[0]) + sh2_ref[0]
    h2_ref[...] = h2
    h_hi = h2.astype(BF16)
    h_lo = (h2 - h_hi.astype(F32)).astype(BF16)
    lg = jnp.dot(h_hi, wr_ref[0], preferred_element_type=F32)
    lg = lg + jnp.dot(h_lo, wr_ref[0], preferred_element_type=F32)
    lg = lg + jnp.dot(h_hi, wr_ref[1], preferred_element_type=F32)
    lg_ref[...] = lg + br_ref[...]


def _merge(o_att, o_hy, o_cf, proj2, gate_col0, wb_bf, wo_bf, x2, mod3, norm2_g, wr_pad, br_pad, seq):
    t, d = x2.shape
    bw = o_att.shape[1]
    tm = _tile(seq, 512)
    per_b = seq // tm
    gb = gate_col0 // d
    epad = wr_pad.shape[2]
    row = lambda w: pl.BlockSpec((tm, w), lambda i: (i, 0))
    modc = lambda c: pl.BlockSpec((1, 1, d), lambda i: (i // per_b, 0, c))
    return pl.pallas_call(
        _merge_kernel,
        grid=(t // tm,),
        in_specs=[row(bw), row(bw), row(bw),
                  pl.BlockSpec((tm, d), lambda i: (i, gb)),
                  pl.BlockSpec((tm, d), lambda i: (i, gb + 1)),
                  pl.BlockSpec((tm, d), lambda i: (i, gb + 2)),
                  pl.BlockSpec((3, bw, d), lambda i: (0, 0, 0)),
                  pl.BlockSpec((d, d), lambda i: (0, 0)),
                  row(d),
                  modc(2),
                  pl.BlockSpec((1, d), lambda i: (0, 0)),
                  modc(3), modc(4),
                  pl.BlockSpec((2, d, epad), lambda i: (0, 0, 0)),
                  pl.BlockSpec((1, epad), lambda i: (0, 0))],
        out_specs=[row(d), row(d), row(epad)],
        out_shape=[jax.ShapeDtypeStruct((t, d), F32), jax.ShapeDtypeStruct((t, d), F32),
                   jax.ShapeDtypeStruct((t, epad), F32)],
        compiler_params=_cparams("arbitrary"),
        name="merge",
    )(o_att, o_hy, o_cf, proj2, proj2, proj2, wb_bf, wo_bf, x2, mod3, norm2_g.reshape(1, d),
      mod3, mod3, wr_pad, br_pad)


SEG_ALIGN = 8
SEG_CHUNK = 64


def _router_kernel(lg_ref, tri_ref, low_ref, w_ref, q_ref, tab_ref, cnt_ref, carry_scr, *, n_exp):
    @pl.when(pl.program_id(0) == 0)
    def _():
        carry_scr[...] = jnp.zeros_like(carry_scr)

    lt = lg_ref[...].T[:n_exp]
    eio = lax.broadcasted_iota(I32, lt.shape, 0)
    cur = lt
    vals, idxs = [], []
    for _ in range(TOP_K):
        m = jnp.max(cur, axis=0, keepdims=True)
        ik = jnp.min(jnp.where(cur == m, eio, n_exp), axis=0, keepdims=True)
        vals.append(m)
        idxs.append(ik)
        cur = jnp.where(eio == ik, -jnp.inf, cur)
    ex = [jnp.exp(v - vals[0]) for v in vals]
    den = ex[0] + ex[1] + ex[2] + ex[3]
    onehot = jnp.zeros(lt.shape, F32)
    for ik in idxs:
        onehot = onehot + jnp.where(eio == ik, 1.0, 0.0)
    incl = jnp.dot(onehot.astype(BF16), tri_ref[...], preferred_element_type=F32)
    count = incl[:, incl.shape[1] - 1:]
    padded = jnp.floor((count + (SEG_ALIGN - 1)) * (1.0 / SEG_ALIGN)) * SEG_ALIGN
    padded_b = jnp.broadcast_to(padded, carry_scr.shape)
    seg_start = jnp.dot(low_ref[...], padded_b, precision=lax.Precision.HIGHEST,
                        preferred_element_type=F32)
    local = incl - onehot + seg_start[:, 0:1]
    for k in range(TOP_K):
        w_ref[k:k + 1, :] = ex[k] / den
        q_ref[k:k + 1, :] = jnp.sum(jnp.where(eio == idxs[k], local, 0.0), axis=0, keepdims=True).astype(I32)
    carry = carry_scr[...]
    tab_ref[0, 0] = seg_start
    tab_ref[0, 1] = carry
    tab_ref[0, 2] = padded_b
    carry_scr[...] = carry + padded_b
    cnt_ref[...] = carry + padded_b


def _router(logits_pad, n_exp, tt):
    t, epad = logits_pad.shape
    n_tt = t // tt
    tri = (jnp.arange(tt)[:, None] <= jnp.arange(tt)[None, :]).astype(BF16)
    low = (jnp.arange(n_exp)[None, :] < jnp.arange(n_exp)[:, None]).astype(F32)
    return pl.pallas_call(
        functools.partial(_router_kernel, n_exp=n_exp),
        grid=(n_tt,),
        in_specs=[pl.BlockSpec((tt, epad), lambda i: (i, 0)),
                  pl.BlockSpec((tt, tt), lambda i: (0, 0)),
                  pl.BlockSpec((n_exp, n_exp), lambda i: (0, 0))],
        out_specs=[pl.BlockSpec((TOP_K, tt), lambda i: (0, i)),
                   pl.BlockSpec((TOP_K, tt), lambda i: (0, i)),
                   pl.BlockSpec((1, 3, n_exp, LANES), lambda i: (i, 0, 0, 0)),
                   pl.BlockSpec((n_exp, LANES), lambda i: (0, 0))],
        out_shape=[jax.ShapeDtypeStruct((TOP_K, t), F32), jax.ShapeDtypeStruct((TOP_K, t), I32),
                   jax.ShapeDtypeStruct((n_tt, 3, n_exp, LANES), F32),
                   jax.ShapeDtypeStruct((n_exp, LANES), F32)],
        scratch_shapes=[pltpu.VMEM((n_exp, LANES), F32)],
        compiler_params=_cparams("arbitrary"),
        name="router",
    )(logits_pad, tri, low)


def _row_window(ref, base, off, size):
    return ref.at[pl.ds(pl.multiple_of(base + off, SEG_ALIGN), size)]


def _seg_copies(src, src_base, dst, dst_base, length, sem, act):
    n_full = length // SEG_CHUNK

    def full(c, carry):
        off = c * SEG_CHUNK
        act(pltpu.make_async_copy(_row_window(src, src_base, off, SEG_CHUNK),
                                  _row_window(dst, dst_base, off, SEG_CHUNK), sem))
        return carry

    lax.fori_loop(0, n_full, full, 0)
    off = n_full * SEG_CHUNK
    size = SEG_CHUNK // 2
    while size >= SEG_ALIGN:
        bit = (length & size) != 0

        @pl.when(bit)
        def _(off=off, size=size):
            act(pltpu.make_async_copy(_row_window(src, src_base, off, size),
                                      _row_window(dst, dst_base, off, size), sem))

        off = off + jnp.where(bit, size, 0)
        size //= 2


def _start(copy):
    copy.start()


def _wait(copy):
    copy.wait()


def _sort_chunks(rows_total):
    n = 3 if rows_total % (3 * LANES) == 0 else 1
    return n, rows_total // n


def _dispatch_kernel(tab_s, tab_d, tab_m, tail_d, tail_z, n_valid, h_ref, q_ref, xs_hbm, sbuf, zbuf, sem, zsem,
                     *, n_exp):
    i = pl.program_id(0)
    n = pl.num_programs(0)
    slot = i % 2
    tt = h_ref.shape[0]
    rows_total = sbuf.shape[1]
    zrows = zbuf.shape[0]

    def tile_copies(j, s, act):
        def per_expert(e, carry):
            k = j * n_exp + e
            _seg_copies(sbuf.at[s], tab_s[k], xs_hbm, tab_d[k], tab_m[k], sem.at[s], act)
            return carry
        lax.fori_loop(0, n_exp, per_expert, 0)

    def tail_copies(act):
        def per_expert(e, carry):
            off = tail_d[e]
            size = zbuf.shape[0]
            while size >= SEG_ALIGN:
                bit = (tail_z[e] & size) != 0

                @pl.when(bit)
                def _(off=off, size=size):
                    act(pltpu.make_async_copy(zbuf.at[pl.ds(0, size)], _row_window(xs_hbm, off, 0, size),
                                              zsem.at[0]))

                off = off + jnp.where(bit, size, 0)
                size //= 2
            return carry
        lax.fori_loop(0, n_exp, per_expert, 0)

        def per_unused_tile(tile, carry):
            for half in range(2):
                act(pltpu.make_async_copy(zbuf, _row_window(xs_hbm, tile * (2 * zrows), half * zrows, zrows),
                                          zsem.at[0]))
            return carry
        lax.fori_loop(n_valid[0], xs_hbm.shape[0] // (2 * zrows), per_unused_tile, 0)

    @pl.when(i == 0)
    def _():
        zbuf[...] = jnp.zeros_like(zbuf)
        tail_copies(_start)

    @pl.when(i >= 2)
    def _():
        tile_copies(i - 2, slot, _wait)

    hb = h_ref[...].astype(BF16)
    n_chunks, rc = _sort_chunks(rows_total)
    for c in range(n_chunks):
        row = lax.broadcasted_iota(I32, (rc, tt), 0) + c * rc
        pi = jnp.zeros((rc, tt), F32)
        for k in range(TOP_K):
            pi = pi + jnp.where(row == q_ref[k:k + 1, :], 1.0, 0.0)
        sbuf[slot, c * rc:(c + 1) * rc] = jnp.dot(pi.astype(BF16), hb, preferred_element_type=F32)
    tile_copies(i, slot, _start)

    @pl.when(i == n - 1)
    def _():
        @pl.when(i >= 1)
        def _():
            tile_copies(i - 1, 1 - slot, _wait)
        tile_copies(i, slot, _wait)
        tail_copies(_wait)


def _dispatch(h2, q, tabs, tails, n_valid, n_rows, tt, n_exp, zrows):
    t, d = h2.shape
    rows_total = TOP_K * tt + SEG_ALIGN * n_exp
    grid_spec = pltpu.PrefetchScalarGridSpec(
        num_scalar_prefetch=6,
        grid=(t // tt,),
        in_specs=[pl.BlockSpec((tt, d), lambda i, *_: (i, 0)),
                  pl.BlockSpec((TOP_K, tt), lambda i, *_: (0, i))],
        out_specs=pl.BlockSpec(memory_space=pl.ANY),
        scratch_shapes=[pltpu.VMEM((2, rows_total, d), F32), pltpu.VMEM((zrows, d), F32),
                        pltpu.SemaphoreType.DMA((2,)), pltpu.SemaphoreType.DMA((1,))],
    )
    return pl.pallas_call(
        functools.partial(_dispatch_kernel, n_exp=n_exp),
        grid_spec=grid_spec,
        out_shape=jax.ShapeDtypeStruct((n_rows, d), F32),
        compiler_params=_cparams("arbitrary"),
        name="dispatch",
    )(*tabs, *tails, n_valid, h2, q)


def _expert_kernel(te_ref, nv_ref, x_ref, wgu_ref, bgu_ref, wdn_ref, bdn_ref, o_ref, wgu_bf, wdn_bf):
    i = pl.program_id(0)
    ff = wdn_ref.shape[2]
    valid = i < nv_ref[0]
    new_expert = jnp.logical_or(i == 0, te_ref[i] != te_ref[jnp.maximum(i - 1, 0)])

    @pl.when(jnp.logical_and(valid, new_expert))
    def _():
        wgu_bf[...] = wgu_ref[0, 0].astype(BF16)
        wdn_bf[...] = wdn_ref[0, 0].astype(BF16)

    @pl.when(valid)
    def _():
        xb = x_ref[...].astype(BF16)
        gu = jnp.dot(xb, wgu_bf[...], preferred_element_type=F32) + bgu_ref[0, 0]
        g = jnp.minimum(gu[:, :ff], SWIGLU_LIMIT)
        u = jnp.clip(gu[:, ff:], -SWIGLU_LIMIT, SWIGLU_LIMIT)
        hd = (u + 1.0) * (g * jax.nn.sigmoid(SWIGLU_ALPHA * g))
        y = jnp.dot(hd.astype(BF16), wdn_bf[...], preferred_element_type=F32) + bdn_ref[0, 0]
        o_ref[...] = y.astype(o_ref.dtype)

    @pl.when(jnp.logical_not(valid))
    def _():
        o_ref[...] = jnp.zeros_like(o_ref)


def _experts(xs, tile_e, n_valid, w_gu, b_gu, w_down, b_down, layer, tm):
    n_rows, d = xs.shape
    n_tiles = n_rows // tm
    depth, n_exp, _, ff2 = w_gu.shape
    ff = ff2 // 2
    wsel = lambda i, te, nv: (layer, te[i], 0, 0)
    grid_spec = pltpu.PrefetchScalarGridSpec(
        num_scalar_prefetch=2,
        grid=(n_tiles,),
        in_specs=[pl.BlockSpec((tm, d), lambda i, te, nv: (jnp.minimum(i, nv[0] - 1), 0)),
                  pl.BlockSpec((1, 1, d, ff2), wsel),
                  pl.BlockSpec((1, 1, 1, ff2), wsel),
                  pl.BlockSpec((1, 1, ff, d), wsel),
                  pl.BlockSpec((1, 1, 1, d), wsel)],
        out_specs=pl.BlockSpec((tm, d), lambda i, te, nv: (i, 0)),
        scratch_shapes=[pltpu.VMEM((d, ff2), BF16), pltpu.VMEM((ff, d), BF16)],
    )
    return pl.pallas_call(
        _expert_kernel,
        grid_spec=grid_spec,
        out_shape=jax.ShapeDtypeStruct((n_rows, d), F32),
        compiler_params=_cparams("arbitrary"),
        name="experts",
    )(tile_e, n_valid, xs, w_gu, b_gu.reshape(depth, n_exp, 1, ff2), w_down, b_down.reshape(depth, n_exp, 1, d))


def _combine_kernel(tab_s, tab_d, tab_m, ys_hbm, q_ref, w_ref, x_ref, g2_ref, o_ref, cbuf, sem, *, n_exp):
    i = pl.program_id(0)
    n = pl.num_programs(0)
    slot = i % 2
    tt = x_ref.shape[0]
    rows_total = cbuf.shape[1]

    def tile_copies(j, s, act):
        def per_expert(e, carry):
            k = j * n_exp + e
            _seg_copies(ys_hbm, tab_d[k], cbuf.at[s], tab_s[k], tab_m[k], sem.at[s], act)
            return carry
        lax.fori_loop(0, n_exp, per_expert, 0)

    @pl.when(i == 0)
    def _():
        cbuf[...] = jnp.zeros_like(cbuf)
        tile_copies(0, 0, _start)

    @pl.when(i + 1 < n)
    def _():
        tile_copies(i + 1, 1 - slot, _start)

    tile_copies(i, slot, _wait)
    q = q_ref[...]
    w = w_ref[...]
    n_chunks, rc = _sort_chunks(rows_total)
    acc = jnp.zeros(x_ref.shape, F32)
    for c in range(n_chunks):
        col = lax.broadcasted_iota(I32, (tt, rc), 1) + c * rc
        pw = jnp.zeros((tt, rc), F32)
        for k in range(TOP_K):
            pw = pw + jnp.where(col == q[:, k:k + 1], w[:, k:k + 1], 0.0)
        acc = acc + jnp.dot(pw.astype(BF16), cbuf[slot, c * rc:(c + 1) * rc].astype(BF16),
                            preferred_element_type=F32)
    o_ref[...] = x_ref[...] + g2_ref[0] * acc


def _combine(ys, q_t, w_t, tabs, x2, mod3, seq, tt, n_exp):
    t, d = x2.shape
    per_b = seq // tt
    rows_total = TOP_K * tt + SEG_ALIGN * n_exp
    grid_spec = pltpu.PrefetchScalarGridSpec(
        num_scalar_prefetch=3,
        grid=(t // tt,),
        in_specs=[pl.BlockSpec(memory_space=pl.ANY),
                  pl.BlockSpec((tt, TOP_K), lambda i, *_: (i, 0)),
                  pl.BlockSpec((tt, TOP_K), lambda i, *_: (i, 0)),
                  pl.BlockSpec((tt, d), lambda i, *_: (i, 0)),
                  pl.BlockSpec((1, 1, d), lambda i, *_: (i // per_b, 0, 5))],
        out_specs=pl.BlockSpec((tt, d), lambda i, *_: (i, 0)),
        scratch_shapes=[pltpu.VMEM((2, rows_total, d), F32), pltpu.SemaphoreType.DMA((2,))],
    )
    return pl.pallas_call(
        functools.partial(_combine_kernel, n_exp=n_exp),
        grid_spec=grid_spec,
        out_shape=jax.ShapeDtypeStruct((t, d), F32),
        compiler_params=_cparams("arbitrary"),
        name="combine",
    )(*tabs, ys, q_t, w_t, x2, mod3)


def _moe(h2, logits_pad, x2, mod3, w_gu, b_gu, w_down, b_down, layer, seq):
    t, d = x2.shape
    n_exp = w_gu.shape[1]
    tt = _tile(seq, 512)
    tm = _tile(t * TOP_K // n_exp, 512)
    n_tt = t // tt
    w, q, tab, cnt = _router(logits_pad, n_exp, tt)
    tab = tab[:, :, :, 0].astype(I32)
    used = cnt[:, 0].astype(I32)
    group = ((used + tm - 1) // tm) * tm
    gend = jnp.cumsum(group)
    goff = gend - group
    max_rows = TOP_K * t + n_tt * n_exp * (SEG_ALIGN - 1)
    n_tiles = -(-max_rows // tm) + n_exp
    tile_start = jnp.arange(n_tiles, dtype=I32) * tm
    tile_e = jnp.minimum(jnp.sum((tile_start[:, None] >= gend[None, :]).astype(I32), axis=1), n_exp - 1)
    n_valid = (gend[-1:] // tm).astype(I32)
    tabs = (tab[:, 0].reshape(-1), (tab[:, 1] + goff[None, :]).reshape(-1), tab[:, 2].reshape(-1))
    tails = (goff + used, group - used)
    xs = _dispatch(h2, q, tabs, tails, n_valid, n_tiles * tm, tt, n_exp, tm // 2)
    ys = _experts(xs, tile_e, n_valid, w_gu, b_gu, w_down, b_down, layer, tm)
    return _combine(ys, q.T, w.T, tabs, x2, mod3, seq, tt, n_exp)


def kernel(x, c, norm1_g, norm2_g, w_ada, b_ada, w_in, b_in, q_norm_g, k_norm_g, lam_q1, lam_k1, lam_q2, lam_k2, subln_g, hy_conv_w, hy_conv_b, hf_w1, hf_b1, hf_w2, hf_b2, hf_w3, hf_b3, hf_freq, hf_w4, hy_bias, cf_dw_w, cf_dw_b, cf_ln_g, cf_ln_b, w_branch, w_out, w_router, b_router, w_gu, b_gu, w_down, b_down):
    bsz, seq, d = x.shape
    depth = w_ada.shape[0]
    t = bsz * seq
    bw = d // 2
    att_w = ATT_HEADS * 2 * HEAD_DIM
    hy_col0 = 3 * att_w
    cf_col0 = hy_col0 + 3 * bw
    gate_col0 = cf_col0 + 2 * bw
    n_exp = w_router.shape[2]
    assert att_w == bw and bsz % 2 == 0 and seq % LANES == 0

    hp = dict(hf_w1=hf_w1, hf_b1=hf_b1, hf_w2=hf_w2, hf_b2=hf_b2, hf_w3=hf_w3, hf_b3=hf_b3,
              hf_freq=hf_freq, hf_w4=hf_w4, hy_conv_w=hy_conv_w, hy_conv_b=hy_conv_b, hy_bias=hy_bias)

    pos = jnp.arange(seq, dtype=F32)
    inv = 1.0 / (ROPE_THETA ** (jnp.arange(0, HEAD_DIM, 2, dtype=F32) / HEAD_DIM))
    ang = pos[:, None] * inv[None, :]
    ang = jnp.concatenate([ang, ang], axis=-1)
    sign = jnp.where(jnp.arange(HEAD_DIM) < HEAD_DIM // 2, -1.0, 1.0).astype(F32)
    reps = att_w // HEAD_DIM
    cos_t = jnp.tile(jnp.cos(ang), (1, reps))
    sin_t = jnp.tile(jnp.sin(ang) * sign[None, :], (1, reps))
    lane = jnp.arange(att_w)
    gsum = (lane[:, None] // HEAD_DIM == lane[None, :] // HEAD_DIM).astype(BF16)
    dft = _dft_tables(seq)
    tabs = _hy_tables(seq, bw)
    epad = LANES

    mod_all = _ada_all(c, w_ada, b_ada)
    x2 = x.reshape(t, d)
    for l in range(depth):
        mod3 = mod_all[l].reshape(bsz, 1, 6 * d)
        lam_init = 0.8 - 0.6 * math.exp(-0.3 * l)

        proj2 = _in_proj(x2, mod3, norm1_g[l], w_in[l].astype(BF16), b_in[l], seq)
        proj3 = proj2.reshape(bsz, seq, -1)

        qr, kt = _qk_prep(proj3, cos_t, sin_t, jnp.tile(q_norm_g[l], reps).reshape(1, att_w),
                          jnp.tile(k_norm_g[l], reps).reshape(1, att_w), gsum)
        lamv = jnp.stack([lam_q1[l], lam_k1[l], lam_q2[l], lam_k2[l]])
        o_att = _attention(qr, kt, proj3, lamv, subln_g[l], lam_init)

        o_hy = _hyena(proj3, hp, l, dft, tabs, hy_col0, bw)
        o_cf = _conformer(proj3, cf_dw_w[l], cf_dw_b[l], cf_ln_g[l], cf_ln_b[l], cf_col0, bw)

        wr_pad = jnp.pad(w_router[l], ((0, 0), (0, epad - n_exp)))
        wr_hi = wr_pad.astype(BF16)
        wr_pad = jnp.stack([wr_hi, (wr_pad - wr_hi.astype(F32)).astype(BF16)])
        br_pad = jnp.pad(b_router[l], (0, epad - n_exp)).reshape(1, epad)
        x2, h2, logits = _merge(o_att.reshape(t, bw), o_hy.reshape(t, bw), o_cf.reshape(t, bw), proj2,
                                gate_col0, w_branch[l].astype(BF16), w_out[l].astype(BF16), x2, mod3,
                                norm2_g[l], wr_pad, br_pad, seq)
        x2 = _moe(h2, logits, x2, mod3, w_gu, b_gu, w_down, b_down, l, seq)
    return x2.reshape(bsz, seq, d)
```

```python
import functools
import math

import jax
import jax.numpy as jnp
from jax import lax
from jax.experimental import pallas as pl
from jax.experimental.pallas import tpu as pltpu

F32 = jnp.float32
BF16 = jnp.bfloat16
I32 = jnp.int32

LANES = 128
SUBLANES = 8
VMEM_LIMIT_BYTES = 56 * 1024 * 1024

ATT_HEADS = 4
HEAD_DIM = 64
V_DIM = 2 * HEAD_DIM
ROPE_THETA = 10000.0
HY_BANDS = 16
HY_FAST_DECAY = 0.3
HY_SLOW_DECAY = 1.5
HY_TARGET = 1e-2
HY_SHIFT = 0.05
CF_KERNEL = 31
CF_HALO = 16
TOP_K = 4
SWIGLU_LIMIT = 7.0
SWIGLU_ALPHA = 1.702
EPS = 1e-6
LOG2E = 1.4426950408889634


def _cparams(*sem):
    return pltpu.CompilerParams(dimension_semantics=sem, vmem_limit_bytes=VMEM_LIMIT_BYTES)


def _tile(n, want):
    t = min(n, want)
    while n % t:
        t -= 1
    return t


def _ada_kernel(c_ref, w_ref, b_ref, o_ref):
    c = c_ref[...]
    cond = c * jax.nn.sigmoid(c)
    o_ref[0] = jnp.dot(cond.astype(BF16), w_ref[0].astype(BF16),
                       preferred_element_type=F32) + b_ref[0]


def _ada_all(c, w_ada, b_ada):
    depth, d, n = w_ada.shape
    bsz = c.shape[0]
    tn = _tile(n, 1536)
    return pl.pallas_call(
        _ada_kernel,
        grid=(depth, n // tn),
        in_specs=[pl.BlockSpec((bsz, d), lambda l, j: (0, 0)),
                  pl.BlockSpec((1, d, tn), lambda l, j: (l, 0, j)),
                  pl.BlockSpec((1, 1, tn), lambda l, j: (l, 0, j))],
        out_specs=pl.BlockSpec((1, bsz, tn), lambda l, j: (l, 0, j)),
        out_shape=jax.ShapeDtypeStruct((depth, bsz, n), F32),
        compiler_params=_cparams("arbitrary", "arbitrary"),
        name="ada_mod",
    )(c, w_ada, b_ada.reshape(depth, 1, n))


def _proj_kernel(x_ref, sh_ref, sc_ref, g_ref, w_ref, b_ref, o_ref, h_scr):
    @pl.when(pl.program_id(1) == 0)
    def _():
        x = x_ref[...]
        y = x * lax.rsqrt(jnp.mean(x * x, axis=-1, keepdims=True) + EPS)
        h = (y * g_ref[...]) * (1.0 + sc_ref[0]) + sh_ref[0]
        h_scr[...] = h.astype(BF16)

    acc = jnp.dot(h_scr[...], w_ref[...], preferred_element_type=F32)
    o_ref[...] = (acc + b_ref[...]).astype(o_ref.dtype)


def _in_proj(x2, mod3, norm_g, w_bf, b, seq):
    t, d = x2.shape
    n = w_bf.shape[1]
    tm = _tile(seq, 1024)
    tn = _tile(n, 1792)
    per_b = seq // tm
    return pl.pallas_call(
        _proj_kernel,
        grid=(t // tm, n // tn),
        in_specs=[pl.BlockSpec((tm, d), lambda i, j: (i, 0)),
                  pl.BlockSpec((1, 1, d), lambda i, j: (i // per_b, 0, 0)),
                  pl.BlockSpec((1, 1, d), lambda i, j: (i // per_b, 0, 1)),
                  pl.BlockSpec((1, d), lambda i, j: (0, 0)),
                  pl.BlockSpec((d, tn), lambda i, j: (0, j)),
                  pl.BlockSpec((1, tn), lambda i, j: (0, j))],
        out_specs=pl.BlockSpec((tm, tn), lambda i, j: (i, j)),
        out_shape=jax.ShapeDtypeStruct((t, n), BF16),
        scratch_shapes=[pltpu.VMEM((tm, d), BF16)],
        compiler_params=_cparams("arbitrary", "arbitrary"),
        name="in_proj",
    )(x2, mod3, mod3, norm_g.reshape(1, d), w_bf, b.reshape(1, n))


def _qkprep_kernel(q_ref, k_ref, cos_ref, sin_ref, gq_ref, gk_ref, gs_ref, qo_ref, kto_ref, *, q_scale):
    width = q_ref.shape[-1]
    lane = lax.broadcasted_iota(I32, q_ref.shape[1:], 1)
    first_half = (lane % HEAD_DIM) < (HEAD_DIM // 2)

    def prep(t, g):
        ssq = jnp.dot((t * t).astype(BF16), gs_ref[...], preferred_element_type=F32)
        y = t * lax.rsqrt(ssq * (1.0 / HEAD_DIM) + EPS) * g
        rot = jnp.where(first_half,
                        pltpu.roll(y, width - HEAD_DIM // 2, 1),
                        pltpu.roll(y, HEAD_DIM // 2, 1))
        return y * cos_ref[...] + rot * sin_ref[...]

    q = prep(q_ref[0].astype(F32), gq_ref[...])
    qo_ref[0] = (q * q_scale).astype(BF16)
    k = prep(k_ref[0].astype(F32), gk_ref[...])
    kto_ref[0] = k.T.astype(BF16)


def _qk_prep(proj3, cos_t, sin_t, gq, gk, gsum):
    bsz, seq, _ = proj3.shape
    width = ATT_HEADS * 2 * HEAD_DIM
    tr = _tile(seq, 512)
    q_scale = HEAD_DIM ** -0.5 * LOG2E
    return pl.pallas_call(
        functools.partial(_qkprep_kernel, q_scale=q_scale),
        grid=(bsz, seq // tr),
        in_specs=[pl.BlockSpec((1, tr, width), lambda b, i: (b, i, 0)),
                  pl.BlockSpec((1, tr, width), lambda b, i: (b, i, 1)),
                  pl.BlockSpec((tr, width), lambda b, i: (i, 0)),
                  pl.BlockSpec((tr, width), lambda b, i: (i, 0)),
                  pl.BlockSpec((1, width), lambda b, i: (0, 0)),
                  pl.BlockSpec((1, width), lambda b, i: (0, 0)),
                  pl.BlockSpec((width, width), lambda b, i: (0, 0))],
        out_specs=[pl.BlockSpec((1, tr, width), lambda b, i: (b, i, 0)),
                   pl.BlockSpec((1, width, tr), lambda b, i: (b, 0, i))],
        out_shape=[jax.ShapeDtypeStruct((bsz, seq, width), BF16),
                   jax.ShapeDtypeStruct((bsz, width, seq), BF16)],
        compiler_params=_cparams("arbitrary", "arbitrary"),
        name="qk_prep",
    )(proj3, proj3, cos_t, sin_t, gq, gk, gsum)


def _attn_kernel(q_ref, kt_ref, v_ref, lamv_ref, g_ref, o_ref, *, lam_init):
    lamv = lamv_ref[...]
    lam = (jnp.exp(jnp.sum(lamv[0:1] * lamv[1:2], axis=-1, keepdims=True))
           - jnp.exp(jnp.sum(lamv[2:3] * lamv[3:4], axis=-1, keepdims=True)) + lam_init)
    q = q_ref[0]
    seq = v_ref.shape[1]
    ones = jnp.ones((seq, V_DIM), BF16)
    for h in range(ATT_HEADS):
        v_ext = jnp.concatenate([v_ref[0, :, h * V_DIM:(h + 1) * V_DIM], ones], axis=1)
        outs = []
        for comp in range(2):
            lo = (2 * h + comp) * HEAD_DIM
            s = jnp.dot(q[:, lo:lo + HEAD_DIM], kt_ref[0, lo:lo + HEAD_DIM, :],
                        preferred_element_type=F32)
            p = jnp.exp2((s - jnp.max(s, axis=-1, keepdims=True)).astype(BF16))
            ov = jnp.dot(p, v_ext, preferred_element_type=F32)
            outs.append(ov[:, :V_DIM] * (1.0 / ov[:, V_DIM:V_DIM + 1]))
        o = outs[0] - lam * outs[1]
        o = o * lax.rsqrt(jnp.mean(o * o, axis=-1, keepdims=True) + EPS)
        o_ref[0, :, h * V_DIM:(h + 1) * V_DIM] = (o * g_ref[...] * (1.0 - lam_init)).astype(o_ref.dtype)


def _attention(qr, kt, proj3, lamv, subln_g, lam_init):
    bsz, seq, width = qr.shape
    tq = _tile(seq, 256)
    return pl.pallas_call(
        functools.partial(_attn_kernel, lam_init=lam_init),
        grid=(bsz, seq // tq),
        in_specs=[pl.BlockSpec((1, tq, width), lambda b, i: (b, i, 0)),
                  pl.BlockSpec((1, width, seq), lambda b, i: (b, 0, 0)),
                  pl.BlockSpec((1, seq, width), lambda b, i: (b, 0, 2)),
                  pl.BlockSpec((4, HEAD_DIM), lambda b, i: (0, 0)),
                  pl.BlockSpec((1, V_DIM), lambda b, i: (0, 0))],
        out_specs=pl.BlockSpec((1, tq, width), lambda b, i: (b, i, 0)),
        out_shape=jax.ShapeDtypeStruct((bsz, seq, width), BF16),
        compiler_params=_cparams("arbitrary", "arbitrary"),
        name="diff_attn",
    )(qr, kt, proj3, lamv, subln_g.reshape(1, V_DIM))


def _hyshort_kernel(u_ref, w_ref, b_ref, o_ref):
    u = u_ref[0].astype(F32)
    seq = u.shape[0]
    row = lax.broadcasted_iota(I32, u.shape, 0)
    prev = jnp.where(row == 0, 0.0, pltpu.roll(u, 1, 0))
    nxt = jnp.where(row == seq - 1, 0.0, pltpu.roll(u, seq - 1, 0))
    w = w_ref[...]
    y = w[0:1] * prev + w[1:2] * u + w[2:3] * nxt + b_ref[...]
    o_ref[0, 0] = y.astype(o_ref.dtype)


def _hy_short(proj3, conv_w, conv_b, col0, width):
    bsz, seq, _ = proj3.shape
    cb = LANES
    per = width // cb
    return pl.pallas_call(
        _hyshort_kernel,
        grid=(bsz, 3 * per),
        in_specs=[pl.BlockSpec((1, seq, cb), lambda b, j: (b, 0, col0 // cb + j)),
                  pl.BlockSpec((3, cb), lambda b, j: (0, j)),
                  pl.BlockSpec((1, cb), lambda b, j: (0, j))],
        out_specs=pl.BlockSpec((1, 1, seq, cb), lambda b, j: (j // per, b, 0, j % per)),
        out_shape=jax.ShapeDtypeStruct((3, bsz, seq, width), BF16),
        compiler_params=_cparams("arbitrary", "arbitrary"),
        name="hy_short",
    )(proj3, conv_w, conv_b.reshape(1, 3 * width))


def _filt_kernel(zz_ref, w1_ref, b1_ref, w2_ref, b2_ref, w3_ref, b3_ref, fr_ref,
                 w4f_ref, w4b_ref, wf_ref, wb_ref, o_ref):
    hi = lax.Precision.HIGHEST
    fr = fr_ref[...]
    h = jnp.sin(fr * (jnp.dot(zz_ref[...], w1_ref[...], precision=hi, preferred_element_type=F32) + b1_ref[...]))
    h = jnp.sin(fr * (jnp.dot(h, w2_ref[...], precision=hi, preferred_element_type=F32) + b2_ref[...]))
    h = jnp.sin(fr * (jnp.dot(h, w3_ref[...], precision=hi, preferred_element_type=F32) + b3_ref[...]))
    hb = h.astype(BF16)
    width = wf_ref.shape[-1]
    half = pl.program_id(0)
    first_tile = pl.program_id(1) == 0

    def emit(use_fwd, use_bwd):
        ff = jnp.dot(hb, w4f_ref[...], preferred_element_type=F32) if use_fwd else None
        fb = jnp.dot(hb, w4b_ref[...], preferred_element_type=F32) if use_bwd else None
        for n in range(2):
            sl = slice(n * width, (n + 1) * width)
            val = ff[:, sl] * wf_ref[...] if use_fwd else None
            if use_bwd:
                bwd = fb[:, sl] * wb_ref[...]
                val = bwd if val is None else val + bwd
            o_ref[0, n] = val.astype(o_ref.dtype)

    @pl.when(jnp.logical_and(half == 0, first_tile))
    def _():
        emit(True, True)

    @pl.when(jnp.logical_and(half == 0, jnp.logical_not(first_tile)))
    def _():
        emit(True, False)

    @pl.when(half == 1)
    def _():
        emit(False, True)


def _hy_kernels(tabs, w1p, b1, w2, b2, w3, b3, freq, w4f, w4b, seq, width):
    zz, wf, wb = tabs
    hid = w2.shape[0]
    tr = _tile(seq, 512)
    nt = seq // tr
    emb = zz.shape[1]
    full = lambda shape: pl.BlockSpec(shape, lambda hf, i: tuple(0 for _ in shape))
    return pl.pallas_call(
        _filt_kernel,
        grid=(2, nt),
        in_specs=[pl.BlockSpec((tr, emb), lambda hf, i: (hf * nt + i, 0)),
                  full((emb, hid)), full((1, hid)), full((hid, hid)), full((1, hid)),
                  full((hid, hid)), full((1, hid)), full((1, hid)),
                  full((hid, 2 * width)), full((hid, 2 * width)),
                  pl.BlockSpec((tr, width), lambda hf, i: (hf * nt + i, 0)),
                  pl.BlockSpec((tr, width), lambda hf, i: (hf * nt + i, 0))],
        out_specs=pl.BlockSpec((1, 2, tr, width), lambda hf, i: (hf, 0, i, 0)),
        out_shape=jax.ShapeDtypeStruct((2, 2, seq, width), BF16),
        compiler_params=_cparams("arbitrary", "arbitrary"),
        name="hy_filter",
    )(zz, w1p, b1.reshape(1, hid), w2, b2.reshape(1, hid), w3, b3.reshape(1, hid),
      freq.reshape(1, hid), w4f, w4b, wf, wb)


def _fft_a_kernel(m_ref, xa_ref, xb_ref, o_ref):
    d = jnp.concatenate([xa_ref[0].astype(BF16), xb_ref[0].astype(BF16)], axis=0)
    o_ref[0] = jnp.dot(m_ref[...], d, preferred_element_type=F32).astype(o_ref.dtype)


def _fft_a(m1, xa, xb, pairs, amap, bmap):
    nh, cols = xa.shape[1], xa.shape[2]
    rows = m1.shape[0]
    tc = _tile(cols, 8192)
    return pl.pallas_call(
        _fft_a_kernel,
        grid=(cols // tc, pairs),
        in_specs=[pl.BlockSpec(m1.shape, lambda j, p: (0, 0)),
                  pl.BlockSpec((1, nh, tc), lambda j, p: (amap(p), 0, j)),
                  pl.BlockSpec((1, nh, tc), lambda j, p: (bmap(p), 0, j))],
        out_specs=pl.BlockSpec((1, rows, tc), lambda j, p: (p, 0, j)),
        out_shape=jax.ShapeDtypeStruct((pairs, rows, cols), BF16),
        compiler_params=_cparams("arbitrary", "arbitrary"),
        name="fft_a",
    )(m1, xa, xb)


def _fft_b_kernel(are_ref, aim_ref, g_ref, gi_ref, kf_ref, v_ref):
    kb = are_ref.shape[1]
    half = are_ref.shape[2]
    for q in range(kb):
        d = jnp.concatenate([are_ref[0, q], aim_ref[0, q]], axis=0)
        x = jnp.dot(g_ref[q], d, preferred_element_type=F32)
        xr, xi = x[:half], x[half:]
        kr, ki = kf_ref[q, :half], kf_ref[q, half:]
        p = jnp.concatenate([xr * kr - xi * ki, xr * ki + xi * kr], axis=0).astype(BF16)
        v = jnp.dot(gi_ref[q], p, preferred_element_type=F32)
        v_ref[0, 0, q] = v[:half].astype(v_ref.dtype)
        v_ref[0, 1, q] = v[half:].astype(v_ref.dtype)


def _fft_b(a4, g, gi, kf, order, n2):
    pairs, _, half, width = a4.shape
    kb = _tile(n2, 8)
    nk = n2 // kb
    return pl.pallas_call(
        _fft_b_kernel,
        grid=(nk, pairs),
        in_specs=[pl.BlockSpec((1, kb, half, width), lambda i, p: (p, i, 0, 0)),
                  pl.BlockSpec((1, kb, half, width), lambda i, p: (p, nk + i, 0, 0)),
                  pl.BlockSpec((kb, 2 * half, 2 * half), lambda i, p: (i, 0, 0)),
                  pl.BlockSpec((kb, 2 * half, 2 * half), lambda i, p: (i, 0, 0)),
                  pl.BlockSpec((None, kb, 2 * half, width), lambda i, p: (order, i, 0, 0))],
        out_specs=pl.BlockSpec((1, 2, kb, half, width), lambda i, p: (p, 0, i, 0, 0)),
        out_shape=jax.ShapeDtypeStruct((pairs, 2, n2, half, width), BF16),
        compiler_params=_cparams("arbitrary", "arbitrary"),
        name="fft_b",
    )(a4, a4, g, gi, kf)


def _fft_bf_kernel(are_ref, aim_ref, g_ref, kf_ref):
    kb = are_ref.shape[1]
    for q in range(kb):
        d = jnp.concatenate([are_ref[0, q], aim_ref[0, q]], axis=0)
        kf_ref[0, q] = jnp.dot(g_ref[q], d, preferred_element_type=F32)


def _fft_bf(a4, g, n2):
    pairs, _, half, width = a4.shape
    kb = _tile(n2, 8)
    nk = n2 // kb
    return pl.pallas_call(
        _fft_bf_kernel,
        grid=(nk, pairs),
        in_specs=[pl.BlockSpec((1, kb, half, width), lambda i, p: (p, i, 0, 0)),
                  pl.BlockSpec((1, kb, half, width), lambda i, p: (p, nk + i, 0, 0)),
                  pl.BlockSpec((kb, 2 * half, 2 * half), lambda i, p: (i, 0, 0))],
        out_specs=pl.BlockSpec((1, kb, 2 * half, width), lambda i, p: (p, i, 0, 0)),
        out_shape=jax.ShapeDtypeStruct((pairs, n2, 2 * half, width), F32),
        compiler_params=_cparams("arbitrary", "arbitrary"),
        name="fft_bf",
    )(a4, a4, g)


def _fft_c_kernel(m_ref, v_ref, xg_ref, z_ref, db_ref, o_ref):
    y = jnp.dot(m_ref[...], v_ref[0], preferred_element_type=F32)
    nh = y.shape[0] // 2
    for s in range(2):
        conv = y[s * nh:(s + 1) * nh]
        z = z_ref[s].astype(F32)
        o_ref[s] = (xg_ref[s].astype(F32) * (conv + db_ref[...] * z)).astype(o_ref.dtype)


def _fft_c(m2, v, xg, xg_blk0, z, bsz, dbias_t):
    _, nh, cols = z.shape
    pairs, rows, _ = v.shape
    tc = _tile(cols, 8192)
    return pl.pallas_call(
        _fft_c_kernel,
        grid=(cols // tc, pairs),
        in_specs=[pl.BlockSpec(m2.shape, lambda j, p: (0, 0)),
                  pl.BlockSpec((1, rows, tc), lambda j, p: (p, 0, j)),
                  pl.BlockSpec((2, nh, tc), lambda j, p: (xg_blk0 + p, 0, j)),
                  pl.BlockSpec((2, nh, tc), lambda j, p: (p, 0, j)),
                  pl.BlockSpec((1, tc), lambda j, p: (0, j))],
        out_specs=pl.BlockSpec((2, nh, tc), lambda j, p: (p, 0, j)),
        out_shape=jax.ShapeDtypeStruct((bsz, nh, cols), BF16),
        compiler_params=_cparams("arbitrary", "arbitrary"),
        name="fft_c",
    )(m2, v, xg, z, dbias_t)


def _stack_complex(zr, zi):
    return jnp.concatenate([jnp.concatenate([zr, -zi], axis=-1),
                            jnp.concatenate([zi, zr], axis=-1)], axis=-2)


def _dft_tables(seq):
    n = 2 * seq
    n2 = n // LANES
    nh = n2 // 2
    ar = lambda m: jnp.arange(m, dtype=I32)

    def cis(num, den, sign):
        ang = (2.0 * math.pi / den) * (num % den).astype(F32)
        return jnp.cos(ang), sign * jnp.sin(ang)

    fr, fi = cis(ar(n2)[:, None] * ar(n2)[None, :], n2, -1.0)
    m1_data = _stack_complex(fr[:, :nh], fi[:, :nh])
    m1_filt = jnp.concatenate([fr, fi], axis=0)
    m2 = _stack_complex(fr.T[:nh] / n, -fi.T[:nh] / n)
    k2 = ar(n2)[:, None, None]
    k1 = ar(LANES)[None, :, None]
    n1 = ar(LANES)[None, None, :]
    er, ei = cis(n1 * (n2 * k1 + k2), n, -1.0)
    g = _stack_complex(er, ei)
    gi = _stack_complex(jnp.swapaxes(er, 1, 2), -jnp.swapaxes(ei, 1, 2))
    return (m1_data.astype(BF16), m1_filt.astype(BF16), m2.astype(BF16), g.astype(BF16), gi.astype(BF16), n2)


def _hy_tables(seq, width):
    pos = jnp.arange(seq, dtype=F32)
    t = jnp.linspace(0.0, 1.0, seq, dtype=F32)[:, None]
    bands = jnp.linspace(1e-4, HY_BANDS - 1, HY_BANDS, dtype=F32)
    ang = (2.0 * math.pi / seq) * pos[:, None] * bands[None, :]
    z = jnp.concatenate([t, jnp.cos(ang), -jnp.sin(ang)], axis=-1)
    deltas = jnp.abs(jnp.linspace(math.log(HY_TARGET) / HY_FAST_DECAY,
                                  math.log(HY_TARGET) / HY_SLOW_DECAY, width, dtype=F32))
    window = jnp.exp(-t * deltas[None, :]) + HY_SHIFT
    mirror = jnp.concatenate([jnp.zeros((1,), I32), jnp.arange(seq - 1, 0, -1, dtype=I32)])
    zz = jnp.concatenate([z, z[mirror]], axis=0)
    emb = z.shape[1]
    emb_pad = ((emb + 63) // 64) * 64
    zz = jnp.pad(zz, ((0, 0), (0, emb_pad - emb)))
    zero = jnp.zeros_like(window)
    first = (jnp.arange(seq) == 0)[:, None]
    wf = jnp.concatenate([window, zero], axis=0)
    wb = jnp.concatenate([jnp.where(first, window, 0.0),
                          jnp.where(first, 0.0, window[mirror])], axis=0)
    return zz, wf, wb


def _hyena(proj3, p, l, dft, tabs, col0, width):
    bsz, seq, _ = proj3.shape
    m1_data, m1_filt, m2, g, gi, n2 = dft
    nh = n2 // 2
    cols = LANES * width
    pairs = bsz // 2
    emb_pad = tabs[0].shape[1]
    w1p = jnp.pad(p['hf_w1'][l], ((0, emb_pad - p['hf_w1'].shape[1]), (0, 0)))
    hid = p['hf_w4'].shape[1]
    w4 = p['hf_w4'][l].reshape(hid, 2, 2, width)
    w4f = w4[:, :, 0, :].reshape(hid, 2 * width).astype(BF16)
    w4b = w4[:, :, 1, :].reshape(hid, 2 * width).astype(BF16)
    kern = _hy_kernels(tabs, w1p, p['hf_b1'][l], p['hf_w2'][l], p['hf_b2'][l], p['hf_w3'][l], p['hf_b3'][l],
                       p['hf_freq'][l], w4f, w4b, seq, width)
    kern = kern.reshape(4, nh, cols)
    ka = _fft_a(m1_filt, kern, kern, 2, lambda q: q, lambda q: 2 + q)
    kf = _fft_bf(ka.reshape(2, 2 * n2, LANES, width), g, n2)

    vxx = _hy_short(proj3, p['hy_conv_w'][l], p['hy_conv_b'][l], col0, width)
    vxx = vxx.reshape(3 * bsz, nh, cols)
    z = vxx
    for order in range(2):
        a = _fft_a(m1_data, z, z, pairs, lambda q: 2 * q, lambda q: 2 * q + 1)
        v = _fft_b(a.reshape(pairs, 2 * n2, LANES, width), g, gi, kf, order, n2)
        dbias_t = jnp.tile(p['hy_bias'][l, order], LANES).reshape(1, cols)
        z = _fft_c(m2, v.reshape(pairs, 2 * n2, cols), vxx, (1 + order) * pairs, z, bsz, dbias_t)
    return z.reshape(bsz, seq, width)


CF_ROWS = 64


def _conf_kernel(ap_ref, a_ref, an_ref, gp_ref, g_ref, gn_ref, w_ref, b_ref, lg_ref, lb_ref,
                 o_ref, u_scr, acc_scr):
    i = pl.program_id(1)
    last = pl.num_programs(1) - 1
    tl = a_ref.shape[1]
    width = a_ref.shape[2]

    def glu(a, g):
        return a.astype(F32) * jax.nn.sigmoid(g.astype(F32))

    u_scr[0:CF_HALO] = jnp.where(i > 0, glu(ap_ref[0], gp_ref[0]), 0.0)
    u_scr[CF_HALO:CF_HALO + tl] = glu(a_ref[0], g_ref[0])
    u_scr[CF_HALO + tl:2 * CF_HALO + tl] = jnp.where(i < last, glu(an_ref[0], gn_ref[0]), 0.0)

    shift = CF_HALO - CF_KERNEL // 2

    def rows(r, carry):
        r0 = pl.multiple_of(r * CF_ROWS, CF_ROWS)
        for c in range(width // LANES):
            cs = slice(c * LANES, (c + 1) * LANES)
            win = u_scr[pl.ds(r0, CF_ROWS + 2 * CF_HALO), cs]
            n_win = win.shape[0]
            rolled = [win] + [pltpu.roll(win, n_win - b, 0) for b in range(1, SUBLANES)]
            acc = jnp.zeros((CF_ROWS, LANES), F32)
            for j in range(CF_KERNEL):
                a, b = divmod(shift + j, SUBLANES)
                acc = acc + w_ref[j:j + 1, cs] * rolled[b][a * SUBLANES:a * SUBLANES + CF_ROWS]
            acc_scr[pl.ds(r0, CF_ROWS), cs] = acc
        return carry

    lax.fori_loop(0, tl // CF_ROWS, rows, 0)
    y = acc_scr[...] + b_ref[...]
    mu = jnp.mean(y, axis=-1, keepdims=True)
    yc = y - mu
    var = jnp.mean(yc * yc, axis=-1, keepdims=True)
    yn = yc * lax.rsqrt(var + EPS) * lg_ref[...] + lb_ref[...]
    o_ref[0] = (yn * jax.nn.sigmoid(yn)).astype(o_ref.dtype)


def _conformer(proj3, dw_w, dw_b, ln_g, ln_b, col0, width):
    bsz, seq, _ = proj3.shape
    tl = _tile(seq, 512)
    a_blk = col0 // width
    g_blk = a_blk + 1
    hpt = tl // CF_HALO
    nhalo = seq // CF_HALO

    def cur(blk):
        return pl.BlockSpec((1, tl, width), lambda b, i: (b, i, blk))

    def prev(blk):
        return pl.BlockSpec((1, CF_HALO, width), lambda b, i: (b, jnp.maximum(i * hpt - 1, 0), blk))

    def nxt(blk):
        return pl.BlockSpec((1, CF_HALO, width), lambda b, i: (b, jnp.minimum((i + 1) * hpt, nhalo - 1), blk))

    vec = lambda rows: pl.BlockSpec((rows, width), lambda b, i: (0, 0))
    return pl.pallas_call(
        _conf_kernel,
        grid=(bsz, seq // tl),
        in_specs=[prev(a_blk), cur(a_blk), nxt(a_blk), prev(g_blk), cur(g_blk), nxt(g_blk),
                  vec(CF_KERNEL), vec(1), vec(1), vec(1)],
        out_specs=pl.BlockSpec((1, tl, width), lambda b, i: (b, i, 0)),
        out_shape=jax.ShapeDtypeStruct((bsz, seq, width), BF16),
        scratch_shapes=[pltpu.VMEM((tl + 2 * CF_HALO, width), F32), pltpu.VMEM((tl, width), F32)],
        compiler_params=_cparams("arbitrary", "arbitrary"),
        name="conformer",
    )(proj3, proj3, proj3, proj3, proj3, proj3, dw_w, dw_b.reshape(1, width),
      ln_g.reshape(1, width), ln_b.reshape(1, width))


def _merge_kernel(oa_ref, oh_ref, oc_ref, ga_ref, gh_ref, gc_ref, wb_ref, wo_ref, x_ref, g1_ref,
                  n2_ref, sh2_ref, sc2_ref, wr_ref, br_ref, xo_ref, h2_ref, lg_ref):
    merged = None
    for n, (o_ref, gt_ref) in enumerate(((oa_ref, ga_ref), (oh_ref, gh_ref), (oc_ref, gc_ref))):
        br = jnp.dot(o_ref[...], wb_ref[n], preferred_element_type=F32)
        term = br * jax.nn.sigmoid(gt_ref[...].astype(F32))
        merged = term if merged is None else merged + term
    y = jnp.dot(merged.astype(BF16), wo_ref[...], preferred_element_type=F32)
    xn = x_ref[...] + g1_ref[0] * y
    xo_ref[...] = xn
    hn = xn * lax.rsqrt(jnp.mean(xn * xn, axis=-1, keepdims=True) + EPS)
    h2 = (hn * n2_ref[...]) * (1.0 + sc2_ref[0]) + sh2_ref[0]
    h_hi = h2.astype(BF16)
    h2_ref[...] = h_hi
    h_lo = (h2 - h_hi.astype(F32)).astype(BF16)
    lg = jnp.dot(h_hi, wr_ref[0], preferred_element_type=F32)
    lg = lg + jnp.dot(h_lo, wr_ref[0], preferred_element_type=F32)
    lg = lg + jnp.dot(h_hi, wr_ref[1], preferred_element_type=F32)
    lg_ref[...] = lg + br_ref[...]


def _merge(o_att, o_hy, o_cf, proj2, gate_col0, wb_bf, wo_bf, x2, mod3, norm2_g, wr_pad, br_pad, seq):
    t, d = x2.shape
    bw = o_att.shape[1]
    tm = _tile(seq, 512)
    per_b = seq // tm
    gb = gate_col0 // d
    epad = wr_pad.shape[2]
    row = lambda w: pl.BlockSpec((tm, w), lambda i: (i, 0))
    modc = lambda c: pl.BlockSpec((1, 1, d), lambda i: (i // per_b, 0, c))
    return pl.pallas_call(
        _merge_kernel,
        grid=(t // tm,),
        in_specs=[row(bw), row(bw), row(bw),
                  pl.BlockSpec((tm, d), lambda i: (i, gb)),
                  pl.BlockSpec((tm, d), lambda i: (i, gb + 1)),
                  pl.BlockSpec((tm, d), lambda i: (i, gb + 2)),
                  pl.BlockSpec((3, bw, d), lambda i: (0, 0, 0)),
                  pl.BlockSpec((d, d), lambda i: (0, 0)),
                  row(d),
                  modc(2),
                  pl.BlockSpec((1, d), lambda i: (0, 0)),
                  modc(3), modc(4),
                  pl.BlockSpec((2, d, epad), lambda i: (0, 0, 0)),
                  pl.BlockSpec((1, epad), lambda i: (0, 0))],
        out_specs=[row(d), row(d), row(epad)],
        out_shape=[jax.ShapeDtypeStruct((t, d), F32), jax.ShapeDtypeStruct((t, d), BF16),
                   jax.ShapeDtypeStruct((t, epad), F32)],
        compiler_params=_cparams("arbitrary"),
        name="merge",
    )(o_att, o_hy, o_cf, proj2, proj2, proj2, wb_bf, wo_bf, x2, mod3, norm2_g.reshape(1, d),
      mod3, mod3, wr_pad, br_pad)


SEG_ALIGN = 8
SEG_CHUNK = 64


def _router_kernel(lg_ref, tri_ref, low_ref, qw_ref, tab_ref, cnt_ref, carry_scr, *, n_exp):
    @pl.when(pl.program_id(0) == 0)
    def _():
        carry_scr[...] = jnp.zeros_like(carry_scr)

    lt = lg_ref[...].T[:n_exp]
    eio = lax.broadcasted_iota(I32, lt.shape, 0)
    cur = lt
    vals, idxs = [], []
    for _ in range(TOP_K):
        m = jnp.max(cur, axis=0, keepdims=True)
        ik = jnp.min(jnp.where(cur == m, eio, n_exp), axis=0, keepdims=True)
        vals.append(m)
        idxs.append(ik)
        cur = jnp.where(eio == ik, -jnp.inf, cur)
    ex = [jnp.exp(v - vals[0]) for v in vals]
    den = ex[0] + ex[1] + ex[2] + ex[3]
    onehot = jnp.zeros(lt.shape, F32)
    for ik in idxs:
        onehot = onehot + jnp.where(eio == ik, 1.0, 0.0)
    incl = jnp.dot(onehot.astype(BF16), tri_ref[...], preferred_element_type=F32)
    count = incl[:, incl.shape[1] - 1:]
    padded = jnp.floor((count + (SEG_ALIGN - 1)) * (1.0 / SEG_ALIGN)) * SEG_ALIGN
    padded_b = jnp.broadcast_to(padded, carry_scr.shape)
    seg_start = jnp.dot(low_ref[...], padded_b, precision=lax.Precision.HIGHEST,
                        preferred_element_type=F32)
    local = incl - onehot + seg_start[:, 0:1]
    for k in range(TOP_K):
        qw_ref[k:k + 1, :] = jnp.sum(jnp.where(eio == idxs[k], local, 0.0), axis=0, keepdims=True)
        qw_ref[TOP_K + k:TOP_K + k + 1, :] = ex[k] / den
    carry = carry_scr[...]
    tab_ref[0, 0] = seg_start
    tab_ref[0, 1] = carry
    tab_ref[0, 2] = padded_b
    carry_scr[...] = carry + padded_b
    cnt_ref[...] = carry + padded_b


def _router(logits_pad, n_exp, tt):
    t, epad = logits_pad.shape
    n_tt = t // tt
    tri = (jnp.arange(tt)[:, None] <= jnp.arange(tt)[None, :]).astype(BF16)
    low = (jnp.arange(n_exp)[None, :] < jnp.arange(n_exp)[:, None]).astype(F32)
    return pl.pallas_call(
        functools.partial(_router_kernel, n_exp=n_exp),
        grid=(n_tt,),
        in_specs=[pl.BlockSpec((tt, epad), lambda i: (i, 0)),
                  pl.BlockSpec((tt, tt), lambda i: (0, 0)),
                  pl.BlockSpec((n_exp, n_exp), lambda i: (0, 0))],
        out_specs=[pl.BlockSpec((2 * TOP_K, tt), lambda i: (0, i)),
                   pl.BlockSpec((1, 3, n_exp, LANES), lambda i: (i, 0, 0, 0)),
                   pl.BlockSpec((n_exp, LANES), lambda i: (0, 0))],
        out_shape=[jax.ShapeDtypeStruct((2 * TOP_K, t), F32),
                   jax.ShapeDtypeStruct((n_tt, 3, n_exp, LANES), F32),
                   jax.ShapeDtypeStruct((n_exp, LANES), F32)],
        scratch_shapes=[pltpu.VMEM((n_exp, LANES), F32)],
        compiler_params=_cparams("arbitrary"),
        name="router",
    )(logits_pad, tri, low)


def _row_window(ref, base, off, size):
    return ref.at[pl.ds(pl.multiple_of(base + off, SEG_ALIGN), size)]


def _seg_copies(src, src_base, dst, dst_base, length, sem, act):
    n_full = length // SEG_CHUNK

    def full(c, carry):
        off = c * SEG_CHUNK
        act(pltpu.make_async_copy(_row_window(src, src_base, off, SEG_CHUNK),
                                  _row_window(dst, dst_base, off, SEG_CHUNK), sem))
        return carry

    lax.fori_loop(0, n_full, full, 0)
    off = n_full * SEG_CHUNK
    size = SEG_CHUNK // 2
    while size >= SEG_ALIGN:
        bit = (length & size) != 0

        @pl.when(bit)
        def _(off=off, size=size):
            act(pltpu.make_async_copy(_row_window(src, src_base, off, size),
                                      _row_window(dst, dst_base, off, size), sem))

        off = off + jnp.where(bit, size, 0)
        size //= 2


def _start(copy):
    copy.start()


def _wait(copy):
    copy.wait()


def _sort_chunks(rows_total):
    n = 3 if rows_total % (3 * LANES) == 0 else 1
    return n, rows_total // n


def _dispatch_kernel(tab_s, tab_d, tab_m, tail_d, tail_z, n_valid, h_ref, q_ref, xs_hbm, sbuf, zbuf, sem, zsem,
                     *, n_exp):
    i = pl.program_id(0)
    n = pl.num_programs(0)
    slot = i % 2
    tt = h_ref.shape[0]
    rows_total = sbuf.shape[1]
    zrows = zbuf.shape[0]

    def tile_copies(j, s, act):
        def per_expert(e, carry):
            k = j * n_exp + e
            _seg_copies(sbuf.at[s], tab_s[k], xs_hbm, tab_d[k], tab_m[k], sem.at[s], act)
            return carry
        lax.fori_loop(0, n_exp, per_expert, 0)

    def tail_copies(act):
        def per_expert(e, carry):
            off = tail_d[e]
            size = zbuf.shape[0]
            while size >= SEG_ALIGN:
                bit = (tail_z[e] & size) != 0

                @pl.when(bit)
                def _(off=off, size=size):
                    act(pltpu.make_async_copy(zbuf.at[pl.ds(0, size)], _row_window(xs_hbm, off, 0, size),
                                              zsem.at[0]))

                off = off + jnp.where(bit, size, 0)
                size //= 2
            return carry
        lax.fori_loop(0, n_exp, per_expert, 0)

        def per_unused_tile(tile, carry):
            for half in range(2):
                act(pltpu.make_async_copy(zbuf, _row_window(xs_hbm, tile * (2 * zrows), half * zrows, zrows),
                                          zsem.at[0]))
            return carry
        lax.fori_loop(n_valid[0], xs_hbm.shape[0] // (2 * zrows), per_unused_tile, 0)

    @pl.when(i == 0)
    def _():
        zbuf[...] = jnp.zeros_like(zbuf)
        tail_copies(_start)

    @pl.when(i >= 2)
    def _():
        tile_copies(i - 2, slot, _wait)

    hb = h_ref[...].astype(BF16)
    n_chunks, rc = _sort_chunks(rows_total)
    for c in range(n_chunks):
        row = lax.broadcasted_iota(I32, (rc, tt), 0) + c * rc
        pi = jnp.zeros((rc, tt), F32)
        for k in range(TOP_K):
            pi = jnp.where(row == q_ref[k:k + 1, :].astype(I32), 1.0, pi)
        sbuf[slot, c * rc:(c + 1) * rc] = jnp.dot(pi.astype(BF16), hb, preferred_element_type=F32)
    tile_copies(i, slot, _start)

    @pl.when(i == n - 1)
    def _():
        @pl.when(i >= 1)
        def _():
            tile_copies(i - 1, 1 - slot, _wait)
        tile_copies(i, slot, _wait)
        tail_copies(_wait)


def _dispatch(h2, q, tabs, tails, n_valid, n_rows, tt, n_exp, zrows):
    t, d = h2.shape
    rows_total = TOP_K * tt + SEG_ALIGN * n_exp
    grid_spec = pltpu.PrefetchScalarGridSpec(
        num_scalar_prefetch=6,
        grid=(t // tt,),
        in_specs=[pl.BlockSpec((tt, d), lambda i, *_: (i, 0)),
                  pl.BlockSpec((2 * TOP_K, tt), lambda i, *_: (0, i))],
        out_specs=pl.BlockSpec(memory_space=pl.ANY),
        scratch_shapes=[pltpu.VMEM((2, rows_total, d), F32), pltpu.VMEM((zrows, d), F32),
                        pltpu.SemaphoreType.DMA((2,)), pltpu.SemaphoreType.DMA((1,))],
    )
    return pl.pallas_call(
        functools.partial(_dispatch_kernel, n_exp=n_exp),
        grid_spec=grid_spec,
        out_shape=jax.ShapeDtypeStruct((n_rows, d), F32),
        compiler_params=_cparams("arbitrary"),
        name="dispatch",
    )(*tabs, *tails, n_valid, h2, q)


def _expert_kernel(te_ref, nv_ref, x_ref, wgu_ref, bgu_ref, wdn_ref, bdn_ref, o_ref, wgu_bf, wdn_bf):
    i = pl.program_id(0)
    ff = wdn_ref.shape[2]
    valid = i < nv_ref[0]
    new_expert = jnp.logical_or(i == 0, te_ref[i] != te_ref[jnp.maximum(i - 1, 0)])

    @pl.when(jnp.logical_and(valid, new_expert))
    def _():
        wgu_bf[...] = wgu_ref[0, 0].astype(BF16)
        wdn_bf[...] = wdn_ref[0, 0].astype(BF16)

    @pl.when(valid)
    def _():
        xb = x_ref[...].astype(BF16)
        gu = jnp.dot(xb, wgu_bf[...], preferred_element_type=F32) + bgu_ref[0, 0]
        g = jnp.minimum(gu[:, :ff], SWIGLU_LIMIT)
        u = jnp.clip(gu[:, ff:], -SWIGLU_LIMIT, SWIGLU_LIMIT)
        hd = (u + 1.0) * (g * jax.nn.sigmoid(SWIGLU_ALPHA * g))
        y = jnp.dot(hd.astype(BF16), wdn_bf[...], preferred_element_type=F32) + bdn_ref[0, 0]
        o_ref[...] = y.astype(o_ref.dtype)

    @pl.when(jnp.logical_not(valid))
    def _():
        o_ref[...] = jnp.zeros_like(o_ref)


def _experts(xs, tile_e, n_valid, w_gu, b_gu, w_down, b_down, layer, tm):
    n_rows, d = xs.shape
    n_tiles = n_rows // tm
    depth, n_exp, _, ff2 = w_gu.shape
    ff = ff2 // 2
    wsel = lambda i, te, nv: (layer, te[i], 0, 0)
    grid_spec = pltpu.PrefetchScalarGridSpec(
        num_scalar_prefetch=2,
        grid=(n_tiles,),
        in_specs=[pl.BlockSpec((tm, d), lambda i, te, nv: (jnp.minimum(i, nv[0] - 1), 0)),
                  pl.BlockSpec((1, 1, d, ff2), wsel),
                  pl.BlockSpec((1, 1, 1, ff2), wsel),
                  pl.BlockSpec((1, 1, ff, d), wsel),
                  pl.BlockSpec((1, 1, 1, d), wsel)],
        out_specs=pl.BlockSpec((tm, d), lambda i, te, nv: (i, 0)),
        scratch_shapes=[pltpu.VMEM((d, ff2), BF16), pltpu.VMEM((ff, d), BF16)],
    )
    return pl.pallas_call(
        _expert_kernel,
        grid_spec=grid_spec,
        out_shape=jax.ShapeDtypeStruct((n_rows, d), F32),
        compiler_params=_cparams("arbitrary"),
        name="experts",
    )(tile_e, n_valid, xs, w_gu, b_gu.reshape(depth, n_exp, 1, ff2), w_down, b_down.reshape(depth, n_exp, 1, d))


def _combine_kernel(tab_s, tab_d, tab_m, ys_hbm, qw_ref, x_ref, g2_ref, o_ref, cbuf, sem, *, n_exp):
    i = pl.program_id(0)
    n = pl.num_programs(0)
    slot = i % 2
    tt = x_ref.shape[0]
    rows_total = cbuf.shape[1]

    def tile_copies(j, s, act):
        def per_expert(e, carry):
            k = j * n_exp + e
            _seg_copies(ys_hbm, tab_d[k], cbuf.at[s], tab_s[k], tab_m[k], sem.at[s], act)
            return carry
        lax.fori_loop(0, n_exp, per_expert, 0)

    @pl.when(i == 0)
    def _():
        cbuf[...] = jnp.zeros_like(cbuf)
        tile_copies(0, 0, _start)

    @pl.when(i + 1 < n)
    def _():
        tile_copies(i + 1, 1 - slot, _start)

    tile_copies(i, slot, _wait)
    qw = qw_ref[...]
    qw_t = jnp.concatenate([qw, jnp.zeros((LANES - qw.shape[0], tt), F32)], axis=0).T
    q = qw_t[:, :TOP_K].astype(I32)
    w = qw_t[:, TOP_K:2 * TOP_K]
    n_chunks, rc = _sort_chunks(rows_total)
    acc = jnp.zeros(x_ref.shape, F32)
    for c in range(n_chunks):
        col = lax.broadcasted_iota(I32, (tt, rc), 1) + c * rc
        pw = jnp.zeros((tt, rc), F32)
        for k in range(TOP_K):
            pw = jnp.where(col == q[:, k:k + 1], w[:, k:k + 1], pw)
        acc = acc + jnp.dot(pw.astype(BF16), cbuf[slot, c * rc:(c + 1) * rc].astype(BF16),
                            preferred_element_type=F32)
    o_ref[...] = x_ref[...] + g2_ref[0] * acc


def _combine(ys, qw, tabs, x2, mod3, seq, tt, n_exp):
    t, d = x2.shape
    per_b = seq // tt
    rows_total = TOP_K * tt + SEG_ALIGN * n_exp
    grid_spec = pltpu.PrefetchScalarGridSpec(
        num_scalar_prefetch=3,
        grid=(t // tt,),
        in_specs=[pl.BlockSpec(memory_space=pl.ANY),
                  pl.BlockSpec((2 * TOP_K, tt), lambda i, *_: (0, i)),
                  pl.BlockSpec((tt, d), lambda i, *_: (i, 0)),
                  pl.BlockSpec((1, 1, d), lambda i, *_: (i // per_b, 0, 5))],
        out_specs=pl.BlockSpec((tt, d), lambda i, *_: (i, 0)),
        scratch_shapes=[pltpu.VMEM((2, rows_total, d), F32), pltpu.SemaphoreType.DMA((2,))],
    )
    return pl.pallas_call(
        functools.partial(_combine_kernel, n_exp=n_exp),
        grid_spec=grid_spec,
        out_shape=jax.ShapeDtypeStruct((t, d), F32),
        compiler_params=_cparams("arbitrary"),
        name="combine",
    )(*tabs, ys, qw, x2, mod3)


def _moe(h2, logits_pad, x2, mod3, w_gu, b_gu, w_down, b_down, layer, seq):
    t, d = x2.shape
    n_exp = w_gu.shape[1]
    tt = _tile(seq, 512)
    tm = _tile(t * TOP_K // n_exp, 512)
    n_tt = t // tt
    qw, tab, cnt = _router(logits_pad, n_exp, tt)
    tab = tab[:, :, :, 0].astype(I32)
    used = cnt[:, 0].astype(I32)
    group = ((used + tm - 1) // tm) * tm
    gend = jnp.cumsum(group)
    goff = gend - group
    max_rows = TOP_K * t + n_tt * n_exp * (SEG_ALIGN - 1)
    n_tiles = -(-max_rows // tm) + n_exp
    tile_start = jnp.arange(n_tiles, dtype=I32) * tm
    tile_e = jnp.minimum(jnp.sum((tile_start[:, None] >= gend[None, :]).astype(I32), axis=1), n_exp - 1)
    n_valid = (gend[-1:] // tm).astype(I32)
    tabs = (tab[:, 0].reshape(-1), (tab[:, 1] + goff[None, :]).reshape(-1), tab[:, 2].reshape(-1))
    tails = (goff + used, group - used)
    xs = _dispatch(h2, qw, tabs, tails, n_valid, n_tiles * tm, tt, n_exp, tm // 2)
    ys = _experts(xs, tile_e, n_valid, w_gu, b_gu, w_down, b_down, layer, tm)
    return _combine(ys, qw, tabs, x2, mod3, seq, tt, n_exp)


def kernel(x, c, norm1_g, norm2_g, w_ada, b_ada, w_in, b_in, q_norm_g, k_norm_g, lam_q1, lam_k1, lam_q2, lam_k2, subln_g, hy_conv_w, hy_conv_b, hf_w1, hf_b1, hf_w2, hf_b2, hf_w3, hf_b3, hf_freq, hf_w4, hy_bias, cf_dw_w, cf_dw_b, cf_ln_g, cf_ln_b, w_branch, w_out, w_router, b_router, w_gu, b_gu, w_down, b_down):
    bsz, seq, d = x.shape
    depth = w_ada.shape[0]
    t = bsz * seq
    bw = d // 2
    att_w = ATT_HEADS * 2 * HEAD_DIM
    hy_col0 = 3 * att_w
    cf_col0 = hy_col0 + 3 * bw
    gate_col0 = cf_col0 + 2 * bw
    n_exp = w_router.shape[2]
    assert att_w == bw and bsz % 2 == 0 and seq % LANES == 0

    hp = dict(hf_w1=hf_w1, hf_b1=hf_b1, hf_w2=hf_w2, hf_b2=hf_b2, hf_w3=hf_w3, hf_b3=hf_b3,
              hf_freq=hf_freq, hf_w4=hf_w4, hy_conv_w=hy_conv_w, hy_conv_b=hy_conv_b, hy_bias=hy_bias)

    pos = jnp.arange(seq, dtype=F32)
    inv = 1.0 / (ROPE_THETA ** (jnp.arange(0, HEAD_DIM, 2, dtype=F32) / HEAD_DIM))
    ang = pos[:, None] * inv[None, :]
    ang = jnp.concatenate([ang, ang], axis=-1)
    sign = jnp.where(jnp.arange(HEAD_DIM) < HEAD_DIM // 2, -1.0, 1.0).astype(F32)
    reps = att_w // HEAD_DIM
    cos_t = jnp.tile(jnp.cos(ang), (1, reps))
    sin_t = jnp.tile(jnp.sin(ang) * sign[None, :], (1, reps))
    lane = jnp.arange(att_w)
    gsum = (lane[:, None] // HEAD_DIM == lane[None, :] // HEAD_DIM).astype(BF16)
    dft = _dft_tables(seq)
    tabs = _hy_tables(seq, bw)
    epad = LANES

    mod_all = _ada_all(c, w_ada, b_ada)
    x2 = x.reshape(t, d)
    for l in range(depth):
        mod3 = mod_all[l].reshape(bsz, 1, 6 * d)
        lam_init = 0.8 - 0.6 * math.exp(-0.3 * l)

        proj2 = _in_proj(x2, mod3, norm1_g[l], w_in[l].astype(BF16), b_in[l], seq)
        proj3 = proj2.reshape(bsz, seq, -1)

        qr, kt = _qk_prep(proj3, cos_t, sin_t, jnp.tile(q_norm_g[l], reps).reshape(1, att_w),
                          jnp.tile(k_norm_g[l], reps).reshape(1, att_w), gsum)
        lamv = jnp.stack([lam_q1[l], lam_k1[l], lam_q2[l], lam_k2[l]])
        o_att = _attention(qr, kt, proj3, lamv, subln_g[l], lam_init)

        o_hy = _hyena(proj3, hp, l, dft, tabs, hy_col0, bw)
        o_cf = _conformer(proj3, cf_dw_w[l], cf_dw_b[l], cf_ln_g[l], cf_ln_b[l], cf_col0, bw)

        wr_pad = jnp.pad(w_router[l], ((0, 0), (0, epad - n_exp)))
        wr_hi = wr_pad.astype(BF16)
        wr_pad = jnp.stack([wr_hi, (wr_pad - wr_hi.astype(F32)).astype(BF16)])
        br_pad = jnp.pad(b_router[l], (0, epad - n_exp)).reshape(1, epad)
        x2, h2, logits = _merge(o_att.reshape(t, bw), o_hy.reshape(t, bw), o_cf.reshape(t, bw), proj2,
                                gate_col0, w_branch[l].astype(BF16), w_out[l].astype(BF16), x2, mod3,
                                norm2_g[l], wr_pad, br_pad, seq)
        x2 = _moe(h2, logits, x2, mod3, w_gu, b_gu, w_down, b_down, l, seq)
    return x2.reshape(bsz, seq, d)
```

```python
import functools
import math

import jax
import jax.numpy as jnp
from jax import lax
from jax.experimental import pallas as pl
from jax.experimental.pallas import tpu as pltpu

F32 = jnp.float32
BF16 = jnp.bfloat16
I32 = jnp.int32

LANES = 128
SUBLANES = 8
VMEM_LIMIT_BYTES = 56 * 1024 * 1024

ATT_HEADS = 4
HEAD_DIM = 64
V_DIM = 2 * HEAD_DIM
ROPE_THETA = 10000.0
HY_BANDS = 16
HY_FAST_DECAY = 0.3
HY_SLOW_DECAY = 1.5
HY_TARGET = 1e-2
HY_SHIFT = 0.05
CF_KERNEL = 31
CF_HALO = 16
TOP_K = 4
SWIGLU_LIMIT = 7.0
SWIGLU_ALPHA = 1.702
EPS = 1e-6
LOG2E = 1.4426950408889634

ADA_COLS = 1536
PROJ_ROWS, PROJ_COLS = 1024, 1792
PREP_ROWS = 1024
ATTN_ROWS = 256
HY_SHORT_COLS = 2 * LANES
FILT_ROWS = 512
FFT_COLS = 8192
FFT_K2 = 8
CONF_ROWS_TILE = 512
MERGE_ROWS = 512
MOE_TOKENS = 512
MOE_ROWS = 512


def _cparams(*sem):
    return pltpu.CompilerParams(dimension_semantics=sem, vmem_limit_bytes=VMEM_LIMIT_BYTES)


def _tile(n, want):
    t = min(n, want)
    while n % t:
        t -= 1
    return t


def _ada_kernel(c_ref, w_ref, b_ref, o_ref):
    c = c_ref[...]
    cond = c * jax.nn.sigmoid(c)
    o_ref[0] = jnp.dot(cond.astype(BF16), w_ref[0].astype(BF16),
                       preferred_element_type=F32) + b_ref[0]


def _ada_all(c, w_ada, b_ada):
    depth, d, n = w_ada.shape
    bsz = c.shape[0]
    tn = _tile(n, ADA_COLS)
    return pl.pallas_call(
        _ada_kernel,
        grid=(depth, n // tn),
        in_specs=[pl.BlockSpec((bsz, d), lambda l, j: (0, 0)),
                  pl.BlockSpec((1, d, tn), lambda l, j: (l, 0, j)),
                  pl.BlockSpec((1, 1, tn), lambda l, j: (l, 0, j))],
        out_specs=pl.BlockSpec((1, bsz, tn), lambda l, j: (l, 0, j)),
        out_shape=jax.ShapeDtypeStruct((depth, bsz, n), F32),
        compiler_params=_cparams("arbitrary", "arbitrary"),
        name="ada_mod",
    )(c, w_ada, b_ada.reshape(depth, 1, n))


def _proj_kernel(x_ref, sh_ref, sc_ref, g_ref, w_ref, b_ref, o_ref, h_scr):
    @pl.when(pl.program_id(1) == 0)
    def _():
        x = x_ref[...]
        y = x * lax.rsqrt(jnp.mean(x * x, axis=-1, keepdims=True) + EPS)
        h = (y * g_ref[...]) * (1.0 + sc_ref[0]) + sh_ref[0]
        h_scr[...] = h.astype(BF16)

    acc = jnp.dot(h_scr[...], w_ref[...], preferred_element_type=F32)
    o_ref[...] = (acc + b_ref[...]).astype(o_ref.dtype)


def _in_proj(x2, mod3, norm_g, w_bf, b, seq):
    t, d = x2.shape
    n = w_bf.shape[1]
    tm = _tile(seq, PROJ_ROWS)
    tn = _tile(n, PROJ_COLS)
    per_b = seq // tm
    return pl.pallas_call(
        _proj_kernel,
        grid=(t // tm, n // tn),
        in_specs=[pl.BlockSpec((tm, d), lambda i, j: (i, 0)),
                  pl.BlockSpec((1, 1, d), lambda i, j: (i // per_b, 0, 0)),
                  pl.BlockSpec((1, 1, d), lambda i, j: (i // per_b, 0, 1)),
                  pl.BlockSpec((1, d), lambda i, j: (0, 0)),
                  pl.BlockSpec((d, tn), lambda i, j: (0, j)),
                  pl.BlockSpec((1, tn), lambda i, j: (0, j))],
        out_specs=pl.BlockSpec((tm, tn), lambda i, j: (i, j)),
        out_shape=jax.ShapeDtypeStruct((t, n), BF16),
        scratch_shapes=[pltpu.VMEM((tm, d), BF16)],
        compiler_params=_cparams("arbitrary", "arbitrary"),
        name="in_proj",
    )(x2, mod3, mod3, norm_g.reshape(1, d), w_bf, b.reshape(1, n))


def _qkprep_kernel(q_ref, k_ref, cos_ref, sin_ref, gq_ref, gk_ref, gs_ref, qo_ref, kto_ref, *, q_scale):
    width = q_ref.shape[-1]
    lane = lax.broadcasted_iota(I32, q_ref.shape[1:], 1)
    first_half = (lane % HEAD_DIM) < (HEAD_DIM // 2)

    def prep(t, g):
        ssq = jnp.dot((t * t).astype(BF16), gs_ref[...], preferred_element_type=F32)
        y = t * lax.rsqrt(ssq * (1.0 / HEAD_DIM) + EPS) * g
        rot = jnp.where(first_half,
                        pltpu.roll(y, width - HEAD_DIM // 2, 1),
                        pltpu.roll(y, HEAD_DIM // 2, 1))
        return y * cos_ref[...] + rot * sin_ref[...]

    q = prep(q_ref[0].astype(F32), gq_ref[...])
    qo_ref[0] = (q * q_scale).astype(BF16)
    k = prep(k_ref[0].astype(F32), gk_ref[...])
    kto_ref[0] = k.T.astype(BF16)


def _qk_prep(proj3, cos_t, sin_t, gq, gk, gsum):
    bsz, seq, _ = proj3.shape
    width = ATT_HEADS * 2 * HEAD_DIM
    tr = _tile(seq, PREP_ROWS)
    q_scale = HEAD_DIM ** -0.5 * LOG2E
    return pl.pallas_call(
        functools.partial(_qkprep_kernel, q_scale=q_scale),
        grid=(bsz, seq // tr),
        in_specs=[pl.BlockSpec((1, tr, width), lambda b, i: (b, i, 0)),
                  pl.BlockSpec((1, tr, width), lambda b, i: (b, i, 1)),
                  pl.BlockSpec((tr, width), lambda b, i: (i, 0)),
                  pl.BlockSpec((tr, width), lambda b, i: (i, 0)),
                  pl.BlockSpec((1, width), lambda b, i: (0, 0)),
                  pl.BlockSpec((1, width), lambda b, i: (0, 0)),
                  pl.BlockSpec((width, width), lambda b, i: (0, 0))],
        out_specs=[pl.BlockSpec((1, tr, width), lambda b, i: (b, i, 0)),
                   pl.BlockSpec((1, width, tr), lambda b, i: (b, 0, i))],
        out_shape=[jax.ShapeDtypeStruct((bsz, seq, width), BF16),
                   jax.ShapeDtypeStruct((bsz, width, seq), BF16)],
        compiler_params=_cparams("arbitrary", "arbitrary"),
        name="qk_prep",
    )(proj3, proj3, cos_t, sin_t, gq, gk, gsum)


def _attn_kernel(q_ref, kt_ref, v_ref, lamv_ref, g_ref, o_ref, *, lam_init):
    lamv = lamv_ref[...]
    lam = (jnp.exp(jnp.sum(lamv[0:1] * lamv[1:2], axis=-1, keepdims=True))
           - jnp.exp(jnp.sum(lamv[2:3] * lamv[3:4], axis=-1, keepdims=True)) + lam_init)
    q = q_ref[0]
    seq = v_ref.shape[1]
    ones = jnp.ones((seq, V_DIM), BF16)
    for h in range(ATT_HEADS):
        v_ext = jnp.concatenate([v_ref[0, :, h * V_DIM:(h + 1) * V_DIM], ones], axis=1)
        outs = []
        for comp in range(2):
            lo = (2 * h + comp) * HEAD_DIM
            s = jnp.dot(q[:, lo:lo + HEAD_DIM], kt_ref[0, lo:lo + HEAD_DIM, :],
                        preferred_element_type=F32)
            p = jnp.exp2((s - jnp.max(s, axis=-1, keepdims=True)).astype(BF16))
            ov = jnp.dot(p, v_ext, preferred_element_type=F32)
            outs.append(ov[:, :V_DIM] * (1.0 / ov[:, V_DIM:V_DIM + 1]))
        o = outs[0] - lam * outs[1]
        o = o * lax.rsqrt(jnp.mean(o * o, axis=-1, keepdims=True) + EPS)
        o_ref[0, :, h * V_DIM:(h + 1) * V_DIM] = (o * g_ref[...] * (1.0 - lam_init)).astype(o_ref.dtype)


def _attention(qr, kt, proj3, lamv, subln_g, lam_init):
    bsz, seq, width = qr.shape
    tq = _tile(seq, ATTN_ROWS)
    return pl.pallas_call(
        functools.partial(_attn_kernel, lam_init=lam_init),
        grid=(bsz, seq // tq),
        in_specs=[pl.BlockSpec((1, tq, width), lambda b, i: (b, i, 0)),
                  pl.BlockSpec((1, width, seq), lambda b, i: (b, 0, 0)),
                  pl.BlockSpec((1, seq, width), lambda b, i: (b, 0, 2)),
                  pl.BlockSpec((4, HEAD_DIM), lambda b, i: (0, 0)),
                  pl.BlockSpec((1, V_DIM), lambda b, i: (0, 0))],
        out_specs=pl.BlockSpec((1, tq, width), lambda b, i: (b, i, 0)),
        out_shape=jax.ShapeDtypeStruct((bsz, seq, width), BF16),
        compiler_params=_cparams("arbitrary", "arbitrary"),
        name="diff_attn",
    )(qr, kt, proj3, lamv, subln_g.reshape(1, V_DIM))


def _hyshort_kernel(u_ref, w_ref, b_ref, o_ref):
    u = u_ref[0].astype(F32)
    seq = u.shape[0]
    row = lax.broadcasted_iota(I32, u.shape, 0)
    prev = jnp.where(row == 0, 0.0, pltpu.roll(u, 1, 0))
    nxt = jnp.where(row == seq - 1, 0.0, pltpu.roll(u, seq - 1, 0))
    w = w_ref[...]
    y = w[0:1] * prev + w[1:2] * u + w[2:3] * nxt + b_ref[...]
    o_ref[0, 0] = y.astype(o_ref.dtype)


def _hy_short(proj3, conv_w, conv_b, col0, width):
    bsz, seq, _ = proj3.shape
    cb = _tile(width, HY_SHORT_COLS)
    per = width // cb
    return pl.pallas_call(
        _hyshort_kernel,
        grid=(bsz, 3 * per),
        in_specs=[pl.BlockSpec((1, seq, cb), lambda b, j: (b, 0, col0 // cb + j)),
                  pl.BlockSpec((3, cb), lambda b, j: (0, j)),
                  pl.BlockSpec((1, cb), lambda b, j: (0, j))],
        out_specs=pl.BlockSpec((1, 1, seq, cb), lambda b, j: (j // per, b, 0, j % per)),
        out_shape=jax.ShapeDtypeStruct((3, bsz, seq, width), BF16),
        compiler_params=_cparams("arbitrary", "arbitrary"),
        name="hy_short",
    )(proj3, conv_w, conv_b.reshape(1, 3 * width))


def _filt_kernel(zz_ref, w1_ref, b1_ref, w2_ref, b2_ref, w3_ref, b3_ref, fr_ref,
                 w4f_ref, w4b_ref, wf_ref, wb_ref, o_ref):
    hi = lax.Precision.HIGHEST
    fr = fr_ref[...]
    h = jnp.sin(fr * (jnp.dot(zz_ref[...], w1_ref[...], precision=hi, preferred_element_type=F32) + b1_ref[...]))
    h = jnp.sin(fr * (jnp.dot(h, w2_ref[...], precision=hi, preferred_element_type=F32) + b2_ref[...]))
    h = jnp.sin(fr * (jnp.dot(h, w3_ref[...], precision=hi, preferred_element_type=F32) + b3_ref[...]))
    hb = h.astype(BF16)
    width = wf_ref.shape[-1]
    half = pl.program_id(0)
    first_tile = pl.program_id(1) == 0

    def emit(use_fwd, use_bwd):
        ff = jnp.dot(hb, w4f_ref[...], preferred_element_type=F32) if use_fwd else None
        fb = jnp.dot(hb, w4b_ref[...], preferred_element_type=F32) if use_bwd else None
        for n in range(2):
            sl = slice(n * width, (n + 1) * width)
            val = ff[:, sl] * wf_ref[...] if use_fwd else None
            if use_bwd:
                bwd = fb[:, sl] * wb_ref[...]
                val = bwd if val is None else val + bwd
            o_ref[0, n] = val.astype(o_ref.dtype)

    @pl.when(jnp.logical_and(half == 0, first_tile))
    def _():
        emit(True, True)

    @pl.when(jnp.logical_and(half == 0, jnp.logical_not(first_tile)))
    def _():
        emit(True, False)

    @pl.when(half == 1)
    def _():
        emit(False, True)


def _hy_kernels(tabs, w1p, b1, w2, b2, w3, b3, freq, w4f, w4b, seq, width):
    zz, wf, wb = tabs
    hid = w2.shape[0]
    tr = _tile(seq, FILT_ROWS)
    nt = seq // tr
    emb = zz.shape[1]
    full = lambda shape: pl.BlockSpec(shape, lambda hf, i: tuple(0 for _ in shape))
    return pl.pallas_call(
        _filt_kernel,
        grid=(2, nt),
        in_specs=[pl.BlockSpec((tr, emb), lambda hf, i: (hf * nt + i, 0)),
                  full((emb, hid)), full((1, hid)), full((hid, hid)), full((1, hid)),
                  full((hid, hid)), full((1, hid)), full((1, hid)),
                  full((hid, 2 * width)), full((hid, 2 * width)),
                  pl.BlockSpec((tr, width), lambda hf, i: (hf * nt + i, 0)),
                  pl.BlockSpec((tr, width), lambda hf, i: (hf * nt + i, 0))],
        out_specs=pl.BlockSpec((1, 2, tr, width), lambda hf, i: (hf, 0, i, 0)),
        out_shape=jax.ShapeDtypeStruct((2, 2, seq, width), BF16),
        compiler_params=_cparams("arbitrary", "arbitrary"),
        name="hy_filter",
    )(zz, w1p, b1.reshape(1, hid), w2, b2.reshape(1, hid), w3, b3.reshape(1, hid),
      freq.reshape(1, hid), w4f, w4b, wf, wb)


def _fft_a_kernel(m_ref, xa_ref, xb_ref, o_ref):
    d = jnp.concatenate([xa_ref[0].astype(BF16), xb_ref[0].astype(BF16)], axis=0)
    o_ref[0] = jnp.dot(m_ref[...], d, preferred_element_type=F32).astype(o_ref.dtype)


def _fft_a(m1, xa, xb, pairs, amap, bmap):
    nh, cols = xa.shape[1], xa.shape[2]
    rows = m1.shape[0]
    tc = _tile(cols, FFT_COLS)
    return pl.pallas_call(
        _fft_a_kernel,
        grid=(cols // tc, pairs),
        in_specs=[pl.BlockSpec(m1.shape, lambda j, p: (0, 0)),
                  pl.BlockSpec((1, nh, tc), lambda j, p: (amap(p), 0, j)),
                  pl.BlockSpec((1, nh, tc), lambda j, p: (bmap(p), 0, j))],
        out_specs=pl.BlockSpec((1, rows, tc), lambda j, p: (p, 0, j)),
        out_shape=jax.ShapeDtypeStruct((pairs, rows, cols), BF16),
        compiler_params=_cparams("arbitrary", "arbitrary"),
        name="fft_a",
    )(m1, xa, xb)


def _fft_b_kernel(are_ref, aim_ref, g_ref, gi_ref, kf_ref, v_ref):
    kb = are_ref.shape[1]
    half = are_ref.shape[2]
    for q in range(kb):
        d = jnp.concatenate([are_ref[0, q], aim_ref[0, q]], axis=0)
        x = jnp.dot(g_ref[q], d, preferred_element_type=F32)
        xr, xi = x[:half], x[half:]
        kr, ki = kf_ref[q, :half], kf_ref[q, half:]
        p = jnp.concatenate([xr * kr - xi * ki, xr * ki + xi * kr], axis=0).astype(BF16)
        v = jnp.dot(gi_ref[q], p, preferred_element_type=F32)
        v_ref[0, 0, q] = v[:half].astype(v_ref.dtype)
        v_ref[0, 1, q] = v[half:].astype(v_ref.dtype)


def _fft_b(a4, g, gi, kf, order, n2):
    pairs, _, half, width = a4.shape
    kb = _tile(n2, FFT_K2)
    nk = n2 // kb
    return pl.pallas_call(
        _fft_b_kernel,
        grid=(nk, pairs),
        in_specs=[pl.BlockSpec((1, kb, half, width), lambda i, p: (p, i, 0, 0)),
                  pl.BlockSpec((1, kb, half, width), lambda i, p: (p, nk + i, 0, 0)),
                  pl.BlockSpec((kb, 2 * half, 2 * half), lambda i, p: (i, 0, 0)),
                  pl.BlockSpec((kb, 2 * half, 2 * half), lambda i, p: (i, 0, 0)),
                  pl.BlockSpec((None, kb, 2 * half, width), lambda i, p: (order, i, 0, 0))],
        out_specs=pl.BlockSpec((1, 2, kb, half, width), lambda i, p: (p, 0, i, 0, 0)),
        out_shape=jax.ShapeDtypeStruct((pairs, 2, n2, half, width), BF16),
        compiler_params=_cparams("arbitrary", "arbitrary"),
        name="fft_b",
    )(a4, a4, g, gi, kf)


def _fft_bf_kernel(are_ref, aim_ref, g_ref, kf_ref):
    kb = are_ref.shape[1]
    for q in range(kb):
        d = jnp.concatenate([are_ref[0, q], aim_ref[0, q]], axis=0)
        kf_ref[0, q] = jnp.dot(g_ref[q], d, preferred_element_type=F32)


def _fft_bf(a4, g, n2):
    pairs, _, half, width = a4.shape
    kb = _tile(n2, FFT_K2)
    nk = n2 // kb
    return pl.pallas_call(
        _fft_bf_kernel,
        grid=(nk, pairs),
        in_specs=[pl.BlockSpec((1, kb, half, width), lambda i, p: (p, i, 0, 0)),
                  pl.BlockSpec((1, kb, half, width), lambda i, p: (p, nk + i, 0, 0)),
                  pl.BlockSpec((kb, 2 * half, 2 * half), lambda i, p: (i, 0, 0))],
        out_specs=pl.BlockSpec((1, kb, 2 * half, width), lambda i, p: (p, i, 0, 0)),
        out_shape=jax.ShapeDtypeStruct((pairs, n2, 2 * half, width), F32),
        compiler_params=_cparams("arbitrary", "arbitrary"),
        name="fft_bf",
    )(a4, a4, g)


def _fft_c_kernel(m_ref, v_ref, xg_ref, z_ref, db_ref, o_ref):
    y = jnp.dot(m_ref[...], v_ref[0], preferred_element_type=F32)
    nh = y.shape[0] // 2
    for s in range(2):
        conv = y[s * nh:(s + 1) * nh]
        z = z_ref[s].astype(F32)
        o_ref[s] = (xg_ref[s].astype(F32) * (conv + db_ref[...] * z)).astype(o_ref.dtype)


def _fft_c(m2, v, xg, xg_blk0, z, bsz, dbias_t):
    _, nh, cols = z.shape
    pairs, rows, _ = v.shape
    tc = _tile(cols, FFT_COLS)
    return pl.pallas_call(
        _fft_c_kernel,
        grid=(cols // tc, pairs),
        in_specs=[pl.BlockSpec(m2.shape, lambda j, p: (0, 0)),
                  pl.BlockSpec((1, rows, tc), lambda j, p: (p, 0, j)),
                  pl.BlockSpec((2, nh, tc), lambda j, p: (xg_blk0 + p, 0, j)),
                  pl.BlockSpec((2, nh, tc), lambda j, p: (p, 0, j)),
                  pl.BlockSpec((1, tc), lambda j, p: (0, j))],
        out_specs=pl.BlockSpec((2, nh, tc), lambda j, p: (p, 0, j)),
        out_shape=jax.ShapeDtypeStruct((bsz, nh, cols), BF16),
        compiler_params=_cparams("arbitrary", "arbitrary"),
        name="fft_c",
    )(m2, v, xg, z, dbias_t)


def _stack_complex(zr, zi):
    return jnp.concatenate([jnp.concatenate([zr, -zi], axis=-1),
                            jnp.concatenate([zi, zr], axis=-1)], axis=-2)


def _dft_tables(seq):
    n = 2 * seq
    n2 = n // LANES
    nh = n2 // 2
    ar = lambda m: jnp.arange(m, dtype=I32)

    def cis(num, den, sign):
        ang = (2.0 * math.pi / den) * (num % den).astype(F32)
        return jnp.cos(ang), sign * jnp.sin(ang)

    fr, fi = cis(ar(n2)[:, None] * ar(n2)[None, :], n2, -1.0)
    m1_data = _stack_complex(fr[:, :nh], fi[:, :nh])
    m1_filt = jnp.concatenate([fr, fi], axis=0)
    m2 = _stack_complex(fr.T[:nh] / n, -fi.T[:nh] / n)
    k2 = ar(n2)[:, None, None]
    k1 = ar(LANES)[None, :, None]
    n1 = ar(LANES)[None, None, :]
    er, ei = cis(n1 * (n2 * k1 + k2), n, -1.0)
    g = _stack_complex(er, ei)
    gi = _stack_complex(jnp.swapaxes(er, 1, 2), -jnp.swapaxes(ei, 1, 2))
    return (m1_data.astype(BF16), m1_filt.astype(BF16), m2.astype(BF16), g.astype(BF16), gi.astype(BF16), n2)


def _hy_tables(seq, width):
    pos = jnp.arange(seq, dtype=F32)
    t = jnp.linspace(0.0, 1.0, seq, dtype=F32)[:, None]
    bands = jnp.linspace(1e-4, HY_BANDS - 1, HY_BANDS, dtype=F32)
    ang = (2.0 * math.pi / seq) * pos[:, None] * bands[None, :]
    z = jnp.concatenate([t, jnp.cos(ang), -jnp.sin(ang)], axis=-1)
    deltas = jnp.abs(jnp.linspace(math.log(HY_TARGET) / HY_FAST_DECAY,
                                  math.log(HY_TARGET) / HY_SLOW_DECAY, width, dtype=F32))
    window = jnp.exp(-t * deltas[None, :]) + HY_SHIFT
    mirror = jnp.concatenate([jnp.zeros((1,), I32), jnp.arange(seq - 1, 0, -1, dtype=I32)])
    zz = jnp.concatenate([z, z[mirror]], axis=0)
    emb = z.shape[1]
    emb_pad = ((emb + 63) // 64) * 64
    zz = jnp.pad(zz, ((0, 0), (0, emb_pad - emb)))
    zero = jnp.zeros_like(window)
    first = (jnp.arange(seq) == 0)[:, None]
    wf = jnp.concatenate([window, zero], axis=0)
    wb = jnp.concatenate([jnp.where(first, window, 0.0),
                          jnp.where(first, 0.0, window[mirror])], axis=0)
    return zz, wf, wb


def _hyena(proj3, p, l, dft, tabs, col0, width):
    bsz, seq, _ = proj3.shape
    m1_data, m1_filt, m2, g, gi, n2 = dft
    nh = n2 // 2
    cols = LANES * width
    pairs = bsz // 2
    emb_pad = tabs[0].shape[1]
    w1p = jnp.pad(p['hf_w1'][l], ((0, emb_pad - p['hf_w1'].shape[1]), (0, 0)))
    hid = p['hf_w4'].shape[1]
    w4 = p['hf_w4'][l].reshape(hid, 2, 2, width)
    w4f = w4[:, :, 0, :].reshape(hid, 2 * width).astype(BF16)
    w4b = w4[:, :, 1, :].reshape(hid, 2 * width).astype(BF16)
    kern = _hy_kernels(tabs, w1p, p['hf_b1'][l], p['hf_w2'][l], p['hf_b2'][l], p['hf_w3'][l], p['hf_b3'][l],
                       p['hf_freq'][l], w4f, w4b, seq, width)
    kern = kern.reshape(4, nh, cols)
    ka = _fft_a(m1_filt, kern, kern, 2, lambda q: q, lambda q: 2 + q)
    kf = _fft_bf(ka.reshape(2, 2 * n2, LANES, width), g, n2)

    vxx = _hy_short(proj3, p['hy_conv_w'][l], p['hy_conv_b'][l], col0, width)
    vxx = vxx.reshape(3 * bsz, nh, cols)
    z = vxx
    for order in range(2):
        a = _fft_a(m1_data, z, z, pairs, lambda q: 2 * q, lambda q: 2 * q + 1)
        v = _fft_b(a.reshape(pairs, 2 * n2, LANES, width), g, gi, kf, order, n2)
        dbias_t = jnp.tile(p['hy_bias'][l, order], LANES).reshape(1, cols)
        z = _fft_c(m2, v.reshape(pairs, 2 * n2, cols), vxx, (1 + order) * pairs, z, bsz, dbias_t)
    return z.reshape(bsz, seq, width)


CF_ROWS = 64


def _conf_kernel(ap_ref, a_ref, an_ref, gp_ref, g_ref, gn_ref, w_ref, b_ref, lg_ref, lb_ref,
                 o_ref, u_scr, acc_scr):
    i = pl.program_id(1)
    last = pl.num_programs(1) - 1
    tl = a_ref.shape[1]
    width = a_ref.shape[2]

    def glu(a, g):
        return a.astype(F32) * jax.nn.sigmoid(g.astype(F32))

    u_scr[0:CF_HALO] = jnp.where(i > 0, glu(ap_ref[0], gp_ref[0]), 0.0)
    u_scr[CF_HALO:CF_HALO + tl] = glu(a_ref[0], g_ref[0])
    u_scr[CF_HALO + tl:2 * CF_HALO + tl] = jnp.where(i < last, glu(an_ref[0], gn_ref[0]), 0.0)

    shift = CF_HALO - CF_KERNEL // 2

    def rows(r, carry):
        r0 = pl.multiple_of(r * CF_ROWS, CF_ROWS)
        for c in range(width // LANES):
            cs = slice(c * LANES, (c + 1) * LANES)
            win = u_scr[pl.ds(r0, CF_ROWS + 2 * CF_HALO), cs]
            n_win = win.shape[0]
            rolled = [win] + [pltpu.roll(win, n_win - b, 0) for b in range(1, SUBLANES)]
            acc = jnp.zeros((CF_ROWS, LANES), F32)
            for j in range(CF_KERNEL):
                a, b = divmod(shift + j, SUBLANES)
                acc = acc + w_ref[j:j + 1, cs] * rolled[b][a * SUBLANES:a * SUBLANES + CF_ROWS]
            acc_scr[pl.ds(r0, CF_ROWS), cs] = acc
        return carry

    lax.fori_loop(0, tl // CF_ROWS, rows, 0)
    y = acc_scr[...] + b_ref[...]
    mu = jnp.mean(y, axis=-1, keepdims=True)
    yc = y - mu
    var = jnp.mean(yc * yc, axis=-1, keepdims=True)
    yn = yc * lax.rsqrt(var + EPS) * lg_ref[...] + lb_ref[...]
    o_ref[0] = (yn * jax.nn.sigmoid(yn)).astype(o_ref.dtype)


def _conformer(proj3, dw_w, dw_b, ln_g, ln_b, col0, width):
    bsz, seq, _ = proj3.shape
    tl = _tile(seq, CONF_ROWS_TILE)
    a_blk = col0 // width
    g_blk = a_blk + 1
    hpt = tl // CF_HALO
    nhalo = seq // CF_HALO

    def cur(blk):
        return pl.BlockSpec((1, tl, width), lambda b, i: (b, i, blk))

    def prev(blk):
        return pl.BlockSpec((1, CF_HALO, width), lambda b, i: (b, jnp.maximum(i * hpt - 1, 0), blk))

    def nxt(blk):
        return pl.BlockSpec((1, CF_HALO, width), lambda b, i: (b, jnp.minimum((i + 1) * hpt, nhalo - 1), blk))

    vec = lambda rows: pl.BlockSpec((rows, width), lambda b, i: (0, 0))
    return pl.pallas_call(
        _conf_kernel,
        grid=(bsz, seq // tl),
        in_specs=[prev(a_blk), cur(a_blk), nxt(a_blk), prev(g_blk), cur(g_blk), nxt(g_blk),
                  vec(CF_KERNEL), vec(1), vec(1), vec(1)],
        out_specs=pl.BlockSpec((1, tl, width), lambda b, i: (b, i, 0)),
        out_shape=jax.ShapeDtypeStruct((bsz, seq, width), BF16),
        scratch_shapes=[pltpu.VMEM((tl + 2 * CF_HALO, width), F32), pltpu.VMEM((tl, width), F32)],
        compiler_params=_cparams("arbitrary", "arbitrary"),
        name="conformer",
    )(proj3, proj3, proj3, proj3, proj3, proj3, dw_w, dw_b.reshape(1, width),
      ln_g.reshape(1, width), ln_b.reshape(1, width))


def _merge_kernel(oa_ref, oh_ref, oc_ref, ga_ref, gh_ref, gc_ref, wb_ref, wo_ref, x_ref, g1_ref,
                  n2_ref, sh2_ref, sc2_ref, wr_ref, br_ref, xo_ref, h2_ref, lg_ref):
    merged = None
    for n, (o_ref, gt_ref) in enumerate(((oa_ref, ga_ref), (oh_ref, gh_ref), (oc_ref, gc_ref))):
        br = jnp.dot(o_ref[...], wb_ref[n], preferred_element_type=F32)
        term = br * jax.nn.sigmoid(gt_ref[...].astype(F32))
        merged = term if merged is None else merged + term
    y = jnp.dot(merged.astype(BF16), wo_ref[...], preferred_element_type=F32)
    xn = x_ref[...] + g1_ref[0] * y
    xo_ref[...] = xn
    hn = xn * lax.rsqrt(jnp.mean(xn * xn, axis=-1, keepdims=True) + EPS)
    h2 = (hn * n2_ref[...]) * (1.0 + sc2_ref[0]) + sh2_ref[0]
    h_hi = h2.astype(BF16)
    h2_ref[...] = h_hi
    h_lo = (h2 - h_hi.astype(F32)).astype(BF16)
    lg = jnp.dot(h_hi, wr_ref[0], preferred_element_type=F32)
    lg = lg + jnp.dot(h_lo, wr_ref[0], preferred_element_type=F32)
    lg = lg + jnp.dot(h_hi, wr_ref[1], preferred_element_type=F32)
    lg_ref[...] = lg + br_ref[...]


def _merge(o_att, o_hy, o_cf, proj2, gate_col0, wb_bf, wo_bf, x2, mod3, norm2_g, wr_pad, br_pad, seq):
    t, d = x2.shape
    bw = o_att.shape[1]
    tm = _tile(seq, MERGE_ROWS)
    per_b = seq // tm
    gb = gate_col0 // d
    epad = wr_pad.shape[2]
    row = lambda w: pl.BlockSpec((tm, w), lambda i: (i, 0))
    modc = lambda c: pl.BlockSpec((1, 1, d), lambda i: (i // per_b, 0, c))
    return pl.pallas_call(
        _merge_kernel,
        grid=(t // tm,),
        in_specs=[row(bw), row(bw), row(bw),
                  pl.BlockSpec((tm, d), lambda i: (i, gb)),
                  pl.BlockSpec((tm, d), lambda i: (i, gb + 1)),
                  pl.BlockSpec((tm, d), lambda i: (i, gb + 2)),
                  pl.BlockSpec((3, bw, d), lambda i: (0, 0, 0)),
                  pl.BlockSpec((d, d), lambda i: (0, 0)),
                  row(d),
                  modc(2),
                  pl.BlockSpec((1, d), lambda i: (0, 0)),
                  modc(3), modc(4),
                  pl.BlockSpec((2, d, epad), lambda i: (0, 0, 0)),
                  pl.BlockSpec((1, epad), lambda i: (0, 0))],
        out_specs=[row(d), row(d), row(epad)],
        out_shape=[jax.ShapeDtypeStruct((t, d), F32), jax.ShapeDtypeStruct((t, d), BF16),
                   jax.ShapeDtypeStruct((t, epad), F32)],
        compiler_params=_cparams("arbitrary"),
        name="merge",
    )(o_att, o_hy, o_cf, proj2, proj2, proj2, wb_bf, wo_bf, x2, mod3, norm2_g.reshape(1, d),
      mod3, mod3, wr_pad, br_pad)


SEG_ALIGN = 8
SEG_CHUNK = 64


def _router_kernel(lg_ref, tri_ref, low_ref, qw_ref, tab_ref, cnt_ref, carry_scr, *, n_exp):
    @pl.when(pl.program_id(0) == 0)
    def _():
        carry_scr[...] = jnp.zeros_like(carry_scr)

    lt = lg_ref[...].T[:n_exp]
    eio = lax.broadcasted_iota(I32, lt.shape, 0)
    cur = lt
    vals, idxs = [], []
    for _ in range(TOP_K):
        m = jnp.max(cur, axis=0, keepdims=True)
        ik = jnp.min(jnp.where(cur == m, eio, n_exp), axis=0, keepdims=True)
        vals.append(m)
        idxs.append(ik)
        cur = jnp.where(eio == ik, -jnp.inf, cur)
    ex = [jnp.exp(v - vals[0]) for v in vals]
    den = ex[0] + ex[1] + ex[2] + ex[3]
    onehot = jnp.zeros(lt.shape, F32)
    for ik in idxs:
        onehot = onehot + jnp.where(eio == ik, 1.0, 0.0)
    incl = jnp.dot(onehot.astype(BF16), tri_ref[...], preferred_element_type=F32)
    count = incl[:, incl.shape[1] - 1:]
    padded = jnp.floor((count + (SEG_ALIGN - 1)) * (1.0 / SEG_ALIGN)) * SEG_ALIGN
    padded_b = jnp.broadcast_to(padded, carry_scr.shape)
    seg_start = jnp.dot(low_ref[...], padded_b, precision=lax.Precision.HIGHEST,
                        preferred_element_type=F32)
    local = incl - onehot + seg_start[:, 0:1]
    for k in range(TOP_K):
        qw_ref[k:k + 1, :] = jnp.sum(jnp.where(eio == idxs[k], local, 0.0), axis=0, keepdims=True)
        qw_ref[TOP_K + k:TOP_K + k + 1, :] = ex[k] / den
    carry = carry_scr[...]
    tab_ref[0, 0] = seg_start
    tab_ref[0, 1] = carry
    tab_ref[0, 2] = padded_b
    carry_scr[...] = carry + padded_b
    cnt_ref[...] = carry + padded_b


def _router(logits_pad, n_exp, tt):
    t, epad = logits_pad.shape
    n_tt = t // tt
    tri = (jnp.arange(tt)[:, None] <= jnp.arange(tt)[None, :]).astype(BF16)
    low = (jnp.arange(n_exp)[None, :] < jnp.arange(n_exp)[:, None]).astype(F32)
    return pl.pallas_call(
        functools.partial(_router_kernel, n_exp=n_exp),
        grid=(n_tt,),
        in_specs=[pl.BlockSpec((tt, epad), lambda i: (i, 0)),
                  pl.BlockSpec((tt, tt), lambda i: (0, 0)),
                  pl.BlockSpec((n_exp, n_exp), lambda i: (0, 0))],
        out_specs=[pl.BlockSpec((2 * TOP_K, tt), lambda i: (0, i)),
                   pl.BlockSpec((1, 3, n_exp, LANES), lambda i: (i, 0, 0, 0)),
                   pl.BlockSpec((n_exp, LANES), lambda i: (0, 0))],
        out_shape=[jax.ShapeDtypeStruct((2 * TOP_K, t), F32),
                   jax.ShapeDtypeStruct((n_tt, 3, n_exp, LANES), F32),
                   jax.ShapeDtypeStruct((n_exp, LANES), F32)],
        scratch_shapes=[pltpu.VMEM((n_exp, LANES), F32)],
        compiler_params=_cparams("arbitrary"),
        name="router",
    )(logits_pad, tri, low)


def _row_window(ref, base, off, size):
    return ref.at[pl.ds(pl.multiple_of(base + off, SEG_ALIGN), size)]


def _seg_copies(src, src_base, dst, dst_base, length, sem, act):
    n_full = length // SEG_CHUNK

    def full(c, carry):
        off = c * SEG_CHUNK
        act(pltpu.make_async_copy(_row_window(src, src_base, off, SEG_CHUNK),
                                  _row_window(dst, dst_base, off, SEG_CHUNK), sem))
        return carry

    lax.fori_loop(0, n_full, full, 0)
    off = n_full * SEG_CHUNK
    size = SEG_CHUNK // 2
    while size >= SEG_ALIGN:
        bit = (length & size) != 0

        @pl.when(bit)
        def _(off=off, size=size):
            act(pltpu.make_async_copy(_row_window(src, src_base, off, size),
                                      _row_window(dst, dst_base, off, size), sem))

        off = off + jnp.where(bit, size, 0)
        size //= 2


def _start(copy):
    copy.start()


def _wait(copy):
    copy.wait()


def _sort_chunks(rows_total):
    n = 3 if rows_total % (3 * LANES) == 0 else 1
    return n, rows_total // n


def _dispatch_kernel(tab_s, tab_d, tab_m, tail_d, tail_z, n_valid, h_ref, q_ref, xs_hbm, sbuf, zbuf, sem, zsem,
                     *, n_exp):
    i = pl.program_id(0)
    n = pl.num_programs(0)
    slot = i % 2
    tt = h_ref.shape[0]
    rows_total = sbuf.shape[1]
    zrows = zbuf.shape[0]

    def tile_copies(j, s, act):
        def per_expert(e, carry):
            k = j * n_exp + e
            _seg_copies(sbuf.at[s], tab_s[k], xs_hbm, tab_d[k], tab_m[k], sem.at[s], act)
            return carry
        lax.fori_loop(0, n_exp, per_expert, 0)

    def tail_copies(act):
        def per_expert(e, carry):
            off = tail_d[e]
            size = zbuf.shape[0]
            while size >= SEG_ALIGN:
                bit = (tail_z[e] & size) != 0

                @pl.when(bit)
                def _(off=off, size=size):
                    act(pltpu.make_async_copy(zbuf.at[pl.ds(0, size)], _row_window(xs_hbm, off, 0, size),
                                              zsem.at[0]))

                off = off + jnp.where(bit, size, 0)
                size //= 2
            return carry
        lax.fori_loop(0, n_exp, per_expert, 0)

        def per_unused_tile(tile, carry):
            for half in range(2):
                act(pltpu.make_async_copy(zbuf, _row_window(xs_hbm, tile * (2 * zrows), half * zrows, zrows),
                                          zsem.at[0]))
            return carry
        lax.fori_loop(n_valid[0], xs_hbm.shape[0] // (2 * zrows), per_unused_tile, 0)

    @pl.when(i == 0)
    def _():
        zbuf[...] = jnp.zeros_like(zbuf)
        tail_copies(_start)

    @pl.when(i >= 2)
    def _():
        tile_copies(i - 2, slot, _wait)

    hb = h_ref[...].astype(BF16)
    n_chunks, rc = _sort_chunks(rows_total)
    for c in range(n_chunks):
        row = lax.broadcasted_iota(I32, (rc, tt), 0) + c * rc
        pi = jnp.zeros((rc, tt), F32)
        for k in range(TOP_K):
            pi = jnp.where(row == q_ref[k:k + 1, :].astype(I32), 1.0, pi)
        sbuf[slot, c * rc:(c + 1) * rc] = jnp.dot(pi.astype(BF16), hb, preferred_element_type=F32)
    tile_copies(i, slot, _start)

    @pl.when(i == n - 1)
    def _():
        @pl.when(i >= 1)
        def _():
            tile_copies(i - 1, 1 - slot, _wait)
        tile_copies(i, slot, _wait)
        tail_copies(_wait)


def _dispatch(h2, q, tabs, tails, n_valid, n_rows, tt, n_exp, zrows):
    t, d = h2.shape
    rows_total = TOP_K * tt + SEG_ALIGN * n_exp
    grid_spec = pltpu.PrefetchScalarGridSpec(
        num_scalar_prefetch=6,
        grid=(t // tt,),
        in_specs=[pl.BlockSpec((tt, d), lambda i, *_: (i, 0)),
                  pl.BlockSpec((2 * TOP_K, tt), lambda i, *_: (0, i))],
        out_specs=pl.BlockSpec(memory_space=pl.ANY),
        scratch_shapes=[pltpu.VMEM((2, rows_total, d), F32), pltpu.VMEM((zrows, d), F32),
                        pltpu.SemaphoreType.DMA((2,)), pltpu.SemaphoreType.DMA((1,))],
    )
    return pl.pallas_call(
        functools.partial(_dispatch_kernel, n_exp=n_exp),
        grid_spec=grid_spec,
        out_shape=jax.ShapeDtypeStruct((n_rows, d), F32),
        compiler_params=_cparams("arbitrary"),
        name="dispatch",
    )(*tabs, *tails, n_valid, h2, q)


def _expert_kernel(te_ref, nv_ref, x_ref, wgu_ref, bgu_ref, wdn_ref, bdn_ref, o_ref, wgu_bf, wdn_bf):
    i = pl.program_id(0)
    ff = wdn_ref.shape[2]
    valid = i < nv_ref[0]
    new_expert = jnp.logical_or(i == 0, te_ref[i] != te_ref[jnp.maximum(i - 1, 0)])

    @pl.when(jnp.logical_and(valid, new_expert))
    def _():
        wgu_bf[...] = wgu_ref[0, 0].astype(BF16)
        wdn_bf[...] = wdn_ref[0, 0].astype(BF16)

    @pl.when(valid)
    def _():
        xb = x_ref[...].astype(BF16)
        gu = jnp.dot(xb, wgu_bf[...], preferred_element_type=F32) + bgu_ref[0, 0]
        g = jnp.minimum(gu[:, :ff], SWIGLU_LIMIT)
        u = jnp.clip(gu[:, ff:], -SWIGLU_LIMIT, SWIGLU_LIMIT)
        hd = (u + 1.0) * (g * jax.nn.sigmoid(SWIGLU_ALPHA * g))
        y = jnp.dot(hd.astype(BF16), wdn_bf[...], preferred_element_type=F32) + bdn_ref[0, 0]
        o_ref[...] = y.astype(o_ref.dtype)

    @pl.when(jnp.logical_not(valid))
    def _():
        o_ref[...] = jnp.zeros_like(o_ref)


def _experts(xs, tile_e, n_valid, w_gu, b_gu, w_down, b_down, layer, tm):
    n_rows, d = xs.shape
    n_tiles = n_rows // tm
    depth, n_exp, _, ff2 = w_gu.shape
    ff = ff2 // 2
    wsel = lambda i, te, nv: (layer, te[i], 0, 0)
    grid_spec = pltpu.PrefetchScalarGridSpec(
        num_scalar_prefetch=2,
        grid=(n_tiles,),
        in_specs=[pl.BlockSpec((tm, d), lambda i, te, nv: (jnp.minimum(i, nv[0] - 1), 0)),
                  pl.BlockSpec((1, 1, d, ff2), wsel),
                  pl.BlockSpec((1, 1, 1, ff2), wsel),
                  pl.BlockSpec((1, 1, ff, d), wsel),
                  pl.BlockSpec((1, 1, 1, d), wsel)],
        out_specs=pl.BlockSpec((tm, d), lambda i, te, nv: (i, 0)),
        scratch_shapes=[pltpu.VMEM((d, ff2), BF16), pltpu.VMEM((ff, d), BF16)],
    )
    return pl.pallas_call(
        _expert_kernel,
        grid_spec=grid_spec,
        out_shape=jax.ShapeDtypeStruct((n_rows, d), F32),
        compiler_params=_cparams("arbitrary"),
        name="experts",
    )(tile_e, n_valid, xs, w_gu, b_gu.reshape(depth, n_exp, 1, ff2), w_down, b_down.reshape(depth, n_exp, 1, d))


def _combine_kernel(tab_s, tab_d, tab_m, ys_hbm, qw_ref, x_ref, g2_ref, o_ref, cbuf, sem, *, n_exp):
    i = pl.program_id(0)
    n = pl.num_programs(0)
    slot = i % 2
    tt = x_ref.shape[0]
    rows_total = cbuf.shape[1]

    def tile_copies(j, s, act):
        def per_expert(e, carry):
            k = j * n_exp + e
            _seg_copies(ys_hbm, tab_d[k], cbuf.at[s], tab_s[k], tab_m[k], sem.at[s], act)
            return carry
        lax.fori_loop(0, n_exp, per_expert, 0)

    @pl.when(i == 0)
    def _():
        cbuf[...] = jnp.zeros_like(cbuf)
        tile_copies(0, 0, _start)

    @pl.when(i + 1 < n)
    def _():
        tile_copies(i + 1, 1 - slot, _start)

    tile_copies(i, slot, _wait)
    qw = qw_ref[...]
    qw_t = jnp.concatenate([qw, jnp.zeros((LANES - qw.shape[0], tt), F32)], axis=0).T
    q = qw_t[:, :TOP_K].astype(I32)
    w = qw_t[:, TOP_K:2 * TOP_K]
    n_chunks, rc = _sort_chunks(rows_total)
    acc = jnp.zeros(x_ref.shape, F32)
    for c in range(n_chunks):
        col = lax.broadcasted_iota(I32, (tt, rc), 1) + c * rc
        pw = jnp.zeros((tt, rc), F32)
        for k in range(TOP_K):
            pw = jnp.where(col == q[:, k:k + 1], w[:, k:k + 1], pw)
        acc = acc + jnp.dot(pw.astype(BF16), cbuf[slot, c * rc:(c + 1) * rc].astype(BF16),
                            preferred_element_type=F32)
    o_ref[...] = x_ref[...] + g2_ref[0] * acc


def _combine(ys, qw, tabs, x2, mod3, seq, tt, n_exp):
    t, d = x2.shape
    per_b = seq // tt
    rows_total = TOP_K * tt + SEG_ALIGN * n_exp
    grid_spec = pltpu.PrefetchScalarGridSpec(
        num_scalar_prefetch=3,
        grid=(t // tt,),
        in_specs=[pl.BlockSpec(memory_space=pl.ANY),
                  pl.BlockSpec((2 * TOP_K, tt), lambda i, *_: (0, i)),
                  pl.BlockSpec((tt, d), lambda i, *_: (i, 0)),
                  pl.BlockSpec((1, 1, d), lambda i, *_: (i // per_b, 0, 5))],
        out_specs=pl.BlockSpec((tt, d), lambda i, *_: (i, 0)),
        scratch_shapes=[pltpu.VMEM((2, rows_total, d), F32), pltpu.SemaphoreType.DMA((2,))],
    )
    return pl.pallas_call(
        functools.partial(_combine_kernel, n_exp=n_exp),
        grid_spec=grid_spec,
        out_shape=jax.ShapeDtypeStruct((t, d), F32),
        compiler_params=_cparams("arbitrary"),
        name="combine",
    )(*tabs, ys, qw, x2, mod3)


def _moe(h2, logits_pad, x2, mod3, w_gu, b_gu, w_down, b_down, layer, seq):
    t, d = x2.shape
    n_exp = w_gu.shape[1]
    tt = _tile(seq, MOE_TOKENS)
    tm = _tile(t * TOP_K // n_exp, MOE_ROWS)
    n_tt = t // tt
    qw, tab, cnt = _router(logits_pad, n_exp, tt)
    tab = tab[:, :, :, 0].astype(I32)
    used = cnt[:, 0].astype(I32)
    group = ((used + tm - 1) // tm) * tm
    gend = jnp.cumsum(group)
    goff = gend - group
    max_rows = TOP_K * t + n_tt * n_exp * (SEG_ALIGN - 1)
    n_tiles = -(-max_rows // tm) + n_exp
    tile_start = jnp.arange(n_tiles, dtype=I32) * tm
    tile_e = jnp.minimum(jnp.sum((tile_start[:, None] >= gend[None, :]).astype(I32), axis=1), n_exp - 1)
    n_valid = (gend[-1:] // tm).astype(I32)
    tabs = (tab[:, 0].reshape(-1), (tab[:, 1] + goff[None, :]).reshape(-1), tab[:, 2].reshape(-1))
    tails = (goff + used, group - used)
    xs = _dispatch(h2, qw, tabs, tails, n_valid, n_tiles * tm, tt, n_exp, tm // 2)
    ys = _experts(xs, tile_e, n_valid, w_gu, b_gu, w_down, b_down, layer, tm)
    return _combine(ys, qw, tabs, x2, mod3, seq, tt, n_exp)


def kernel(x, c, norm1_g, norm2_g, w_ada, b_ada, w_in, b_in, q_norm_g, k_norm_g, lam_q1, lam_k1, lam_q2, lam_k2, subln_g, hy_conv_w, hy_conv_b, hf_w1, hf_b1, hf_w2, hf_b2, hf_w3, hf_b3, hf_freq, hf_w4, hy_bias, cf_dw_w, cf_dw_b, cf_ln_g, cf_ln_b, w_branch, w_out, w_router, b_router, w_gu, b_gu, w_down, b_down):
    bsz, seq, d = x.shape
    depth = w_ada.shape[0]
    t = bsz * seq
    bw = d // 2
    att_w = ATT_HEADS * 2 * HEAD_DIM
    hy_col0 = 3 * att_w
    cf_col0 = hy_col0 + 3 * bw
    gate_col0 = cf_col0 + 2 * bw
    n_exp = w_router.shape[2]
    assert att_w == bw and bsz % 2 == 0 and seq % LANES == 0

    hp = dict(hf_w1=hf_w1, hf_b1=hf_b1, hf_w2=hf_w2, hf_b2=hf_b2, hf_w3=hf_w3, hf_b3=hf_b3,
              hf_freq=hf_freq, hf_w4=hf_w4, hy_conv_w=hy_conv_w, hy_conv_b=hy_conv_b, hy_bias=hy_bias)

    pos = jnp.arange(seq, dtype=F32)
    inv = 1.0 / (ROPE_THETA ** (jnp.arange(0, HEAD_DIM, 2, dtype=F32) / HEAD_DIM))
    ang = pos[:, None] * inv[None, :]
    ang = jnp.concatenate([ang, ang], axis=-1)
    sign = jnp.where(jnp.arange(HEAD_DIM) < HEAD_DIM // 2, -1.0, 1.0).astype(F32)
    reps = att_w // HEAD_DIM
    cos_t = jnp.tile(jnp.cos(ang), (1, reps))
    sin_t = jnp.tile(jnp.sin(ang) * sign[None, :], (1, reps))
    lane = jnp.arange(att_w)
    gsum = (lane[:, None] // HEAD_DIM == lane[None, :] // HEAD_DIM).astype(BF16)
    dft = _dft_tables(seq)
    tabs = _hy_tables(seq, bw)
    epad = LANES

    mod_all = _ada_all(c, w_ada, b_ada)
    x2 = x.reshape(t, d)
    for l in range(depth):
        mod3 = mod_all[l].reshape(bsz, 1, 6 * d)
        lam_init = 0.8 - 0.6 * math.exp(-0.3 * l)

        proj2 = _in_proj(x2, mod3, norm1_g[l], w_in[l].astype(BF16), b_in[l], seq)
        proj3 = proj2.reshape(bsz, seq, -1)

        qr, kt = _qk_prep(proj3, cos_t, sin_t, jnp.tile(q_norm_g[l], reps).reshape(1, att_w),
                          jnp.tile(k_norm_g[l], reps).reshape(1, att_w), gsum)
        lamv = jnp.stack([lam_q1[l], lam_k1[l], lam_q2[l], lam_k2[l]])
        o_att = _attention(qr, kt, proj3, lamv, subln_g[l], lam_init)

        o_hy = _hyena(proj3, hp, l, dft, tabs, hy_col0, bw)
        o_cf = _conformer(proj3, cf_dw_w[l], cf_dw_b[l], cf_ln_g[l], cf_ln_b[l], cf_col0, bw)

        wr_pad = jnp.pad(w_router[l], ((0, 0), (0, epad - n_exp)))
        wr_hi = wr_pad.astype(BF16)
        wr_pad = jnp.stack([wr_hi, (wr_pad - wr_hi.astype(F32)).astype(BF16)])
        br_pad = jnp.pad(b_router[l], (0, epad - n_exp)).reshape(1, epad)
        x2, h2, logits = _merge(o_att.reshape(t, bw), o_hy.reshape(t, bw), o_cf.reshape(t, bw), proj2,
                                gate_col0, w_branch[l].astype(BF16), w_out[l].astype(BF16), x2, mod3,
                                norm2_g[l], wr_pad, br_pad, seq)
        x2 = _moe(h2, logits, x2, mod3, w_gu, b_gu, w_down, b_down, l, seq)
    return x2.reshape(bsz, seq, d)
```

```python
import functools
import math

import jax
import jax.numpy as jnp
from jax import lax
from jax.experimental import pallas as pl
from jax.experimental.pallas import tpu as pltpu

F32 = jnp.float32
BF16 = jnp.bfloat16
I32 = jnp.int32

LANES = 128
SUBLANES = 8
VMEM_LIMIT_BYTES = 56 * 1024 * 1024

ATT_HEADS = 4
HEAD_DIM = 64
V_DIM = 2 * HEAD_DIM
ROPE_THETA = 10000.0
HY_BANDS = 16
HY_FAST_DECAY = 0.3
HY_SLOW_DECAY = 1.5
HY_TARGET = 1e-2
HY_SHIFT = 0.05
CF_KERNEL = 31
CF_HALO = 16
TOP_K = 4
SWIGLU_LIMIT = 7.0
SWIGLU_ALPHA = 1.702
EPS = 1e-6
LOG2E = 1.4426950408889634

ADA_COLS = 1536
PROJ_ROWS, PROJ_COLS = 1024, 1792
PREP_ROWS = 1024
ATTN_ROWS = 256
HY_SHORT_COLS = 2 * LANES
FILT_ROWS = 1024
FFT_COLS = 16384
FFT_K2 = 16
CONF_ROWS_TILE = 1024
MERGE_ROWS = 512
MOE_TOKENS = 512
MOE_ROWS = 512


def _cparams(*sem):
    return pltpu.CompilerParams(dimension_semantics=sem, vmem_limit_bytes=VMEM_LIMIT_BYTES)


def _tile(n, want):
    t = min(n, want)
    while n % t:
        t -= 1
    return t


def _ada_kernel(c_ref, w_ref, b_ref, o_ref):
    c = c_ref[...]
    cond = c * jax.nn.sigmoid(c)
    o_ref[0] = jnp.dot(cond.astype(BF16), w_ref[0].astype(BF16),
                       preferred_element_type=F32) + b_ref[0]


def _ada_all(c, w_ada, b_ada):
    depth, d, n = w_ada.shape
    bsz = c.shape[0]
    tn = _tile(n, ADA_COLS)
    return pl.pallas_call(
        _ada_kernel,
        grid=(depth, n // tn),
        in_specs=[pl.BlockSpec((bsz, d), lambda l, j: (0, 0)),
                  pl.BlockSpec((1, d, tn), lambda l, j: (l, 0, j)),
                  pl.BlockSpec((1, 1, tn), lambda l, j: (l, 0, j))],
        out_specs=pl.BlockSpec((1, bsz, tn), lambda l, j: (l, 0, j)),
        out_shape=jax.ShapeDtypeStruct((depth, bsz, n), F32),
        compiler_params=_cparams("arbitrary", "arbitrary"),
        name="ada_mod",
    )(c, w_ada, b_ada.reshape(depth, 1, n))


def _proj_kernel(x_ref, sh_ref, sc_ref, g_ref, w_ref, b_ref, o_ref, h_scr):
    @pl.when(pl.program_id(1) == 0)
    def _():
        x = x_ref[...]
        y = x * lax.rsqrt(jnp.mean(x * x, axis=-1, keepdims=True) + EPS)
        h = (y * g_ref[...]) * (1.0 + sc_ref[0]) + sh_ref[0]
        h_scr[...] = h.astype(BF16)

    acc = jnp.dot(h_scr[...], w_ref[...], preferred_element_type=F32)
    o_ref[...] = (acc + b_ref[...]).astype(o_ref.dtype)


def _in_proj(x2, mod3, norm_g, w_bf, b, seq):
    t, d = x2.shape
    n = w_bf.shape[1]
    tm = _tile(seq, PROJ_ROWS)
    tn = _tile(n, PROJ_COLS)
    per_b = seq // tm
    return pl.pallas_call(
        _proj_kernel,
        grid=(t // tm, n // tn),
        in_specs=[pl.BlockSpec((tm, d), lambda i, j: (i, 0)),
                  pl.BlockSpec((1, 1, d), lambda i, j: (i // per_b, 0, 0)),
                  pl.BlockSpec((1, 1, d), lambda i, j: (i // per_b, 0, 1)),
                  pl.BlockSpec((1, d), lambda i, j: (0, 0)),
                  pl.BlockSpec((d, tn), lambda i, j: (0, j)),
                  pl.BlockSpec((1, tn), lambda i, j: (0, j))],
        out_specs=pl.BlockSpec((tm, tn), lambda i, j: (i, j)),
        out_shape=jax.ShapeDtypeStruct((t, n), BF16),
        scratch_shapes=[pltpu.VMEM((tm, d), BF16)],
        compiler_params=_cparams("arbitrary", "arbitrary"),
        name="in_proj",
    )(x2, mod3, mod3, norm_g.reshape(1, d), w_bf, b.reshape(1, n))


def _qkprep_kernel(q_ref, k_ref, cos_ref, sin_ref, gq_ref, gk_ref, gs_ref, qo_ref, kto_ref, *, q_scale):
    width = q_ref.shape[-1]
    lane = lax.broadcasted_iota(I32, q_ref.shape[1:], 1)
    first_half = (lane % HEAD_DIM) < (HEAD_DIM // 2)

    def prep(t, g):
        ssq = jnp.dot((t * t).astype(BF16), gs_ref[...], preferred_element_type=F32)
        y = t * lax.rsqrt(ssq * (1.0 / HEAD_DIM) + EPS) * g
        rot = jnp.where(first_half,
                        pltpu.roll(y, width - HEAD_DIM // 2, 1),
                        pltpu.roll(y, HEAD_DIM // 2, 1))
        return y * cos_ref[...] + rot * sin_ref[...]

    q = prep(q_ref[0].astype(F32), gq_ref[...])
    qo_ref[0] = (q * q_scale).astype(BF16)
    k = prep(k_ref[0].astype(F32), gk_ref[...])
    kto_ref[0] = k.T.astype(BF16)


def _qk_prep(proj3, cos_t, sin_t, gq, gk, gsum):
    bsz, seq, _ = proj3.shape
    width = ATT_HEADS * 2 * HEAD_DIM
    tr = _tile(seq, PREP_ROWS)
    q_scale = HEAD_DIM ** -0.5 * LOG2E
    return pl.pallas_call(
        functools.partial(_qkprep_kernel, q_scale=q_scale),
        grid=(bsz, seq // tr),
        in_specs=[pl.BlockSpec((1, tr, width), lambda b, i: (b, i, 0)),
                  pl.BlockSpec((1, tr, width), lambda b, i: (b, i, 1)),
                  pl.BlockSpec((tr, width), lambda b, i: (i, 0)),
                  pl.BlockSpec((tr, width), lambda b, i: (i, 0)),
                  pl.BlockSpec((1, width), lambda b, i: (0, 0)),
                  pl.BlockSpec((1, width), lambda b, i: (0, 0)),
                  pl.BlockSpec((width, width), lambda b, i: (0, 0))],
        out_specs=[pl.BlockSpec((1, tr, width), lambda b, i: (b, i, 0)),
                   pl.BlockSpec((1, width, tr), lambda b, i: (b, 0, i))],
        out_shape=[jax.ShapeDtypeStruct((bsz, seq, width), BF16),
                   jax.ShapeDtypeStruct((bsz, width, seq), BF16)],
        compiler_params=_cparams("arbitrary", "arbitrary"),
        name="qk_prep",
    )(proj3, proj3, cos_t, sin_t, gq, gk, gsum)


def _attn_kernel(q_ref, kt_ref, v_ref, lamv_ref, g_ref, o_ref, *, lam_init):
    lamv = lamv_ref[...]
    lam = (jnp.exp(jnp.sum(lamv[0:1] * lamv[1:2], axis=-1, keepdims=True))
           - jnp.exp(jnp.sum(lamv[2:3] * lamv[3:4], axis=-1, keepdims=True)) + lam_init)
    q = q_ref[0]
    seq = v_ref.shape[1]
    ones = jnp.ones((seq, V_DIM), BF16)
    for h in range(ATT_HEADS):
        v_ext = jnp.concatenate([v_ref[0, :, h * V_DIM:(h + 1) * V_DIM], ones], axis=1)
        outs = []
        for comp in range(2):
            lo = (2 * h + comp) * HEAD_DIM
            s = jnp.dot(q[:, lo:lo + HEAD_DIM], kt_ref[0, lo:lo + HEAD_DIM, :],
                        preferred_element_type=F32)
            p = jnp.exp2((s - jnp.max(s, axis=-1, keepdims=True)).astype(BF16))
            ov = jnp.dot(p, v_ext, preferred_element_type=F32)
            outs.append(ov[:, :V_DIM] * (1.0 / ov[:, V_DIM:V_DIM + 1]))
        o = outs[0] - lam * outs[1]
        o = o * lax.rsqrt(jnp.mean(o * o, axis=-1, keepdims=True) + EPS)
        o_ref[0, :, h * V_DIM:(h + 1) * V_DIM] = (o * g_ref[...] * (1.0 - lam_init)).astype(o_ref.dtype)


def _attention(qr, kt, proj3, lamv, subln_g, lam_init):
    bsz, seq, width = qr.shape
    tq = _tile(seq, ATTN_ROWS)
    return pl.pallas_call(
        functools.partial(_attn_kernel, lam_init=lam_init),
        grid=(bsz, seq // tq),
        in_specs=[pl.BlockSpec((1, tq, width), lambda b, i: (b, i, 0)),
                  pl.BlockSpec((1, width, seq), lambda b, i: (b, 0, 0)),
                  pl.BlockSpec((1, seq, width), lambda b, i: (b, 0, 2)),
                  pl.BlockSpec((4, HEAD_DIM), lambda b, i: (0, 0)),
                  pl.BlockSpec((1, V_DIM), lambda b, i: (0, 0))],
        out_specs=pl.BlockSpec((1, tq, width), lambda b, i: (b, i, 0)),
        out_shape=jax.ShapeDtypeStruct((bsz, seq, width), BF16),
        compiler_params=_cparams("arbitrary", "arbitrary"),
        name="diff_attn",
    )(qr, kt, proj3, lamv, subln_g.reshape(1, V_DIM))


def _hyshort_kernel(u_ref, w_ref, b_ref, o_ref):
    u = u_ref[0].astype(F32)
    seq = u.shape[0]
    row = lax.broadcasted_iota(I32, u.shape, 0)
    prev = jnp.where(row == 0, 0.0, pltpu.roll(u, 1, 0))
    nxt = jnp.where(row == seq - 1, 0.0, pltpu.roll(u, seq - 1, 0))
    w = w_ref[...]
    y = w[0:1] * prev + w[1:2] * u + w[2:3] * nxt + b_ref[...]
    o_ref[0, 0] = y.astype(o_ref.dtype)


def _hy_short(proj3, conv_w, conv_b, col0, width):
    bsz, seq, _ = proj3.shape
    cb = _tile(width, HY_SHORT_COLS)
    per = width // cb
    return pl.pallas_call(
        _hyshort_kernel,
        grid=(bsz, 3 * per),
        in_specs=[pl.BlockSpec((1, seq, cb), lambda b, j: (b, 0, col0 // cb + j)),
                  pl.BlockSpec((3, cb), lambda b, j: (0, j)),
                  pl.BlockSpec((1, cb), lambda b, j: (0, j))],
        out_specs=pl.BlockSpec((1, 1, seq, cb), lambda b, j: (j // per, b, 0, j % per)),
        out_shape=jax.ShapeDtypeStruct((3, bsz, seq, width), BF16),
        compiler_params=_cparams("arbitrary", "arbitrary"),
        name="hy_short",
    )(proj3, conv_w, conv_b.reshape(1, 3 * width))


def _filt_kernel(zz_ref, w1_ref, b1_ref, w2_ref, b2_ref, w3_ref, b3_ref, fr_ref,
                 w4f_ref, w4b_ref, wf_ref, wb_ref, o_ref):
    hi = lax.Precision.HIGHEST
    fr = fr_ref[...]
    h = jnp.sin(fr * (jnp.dot(zz_ref[...], w1_ref[...], precision=hi, preferred_element_type=F32) + b1_ref[...]))
    h = jnp.sin(fr * (jnp.dot(h, w2_ref[...], precision=hi, preferred_element_type=F32) + b2_ref[...]))
    h = jnp.sin(fr * (jnp.dot(h, w3_ref[...], precision=hi, preferred_element_type=F32) + b3_ref[...]))
    hb = h.astype(BF16)
    width = wf_ref.shape[-1]
    half = pl.program_id(0)
    first_tile = pl.program_id(1) == 0

    def emit(use_fwd, use_bwd):
        ff = jnp.dot(hb, w4f_ref[...], preferred_element_type=F32) if use_fwd else None
        fb = jnp.dot(hb, w4b_ref[...], preferred_element_type=F32) if use_bwd else None
        for n in range(2):
            sl = slice(n * width, (n + 1) * width)
            val = ff[:, sl] * wf_ref[...] if use_fwd else None
            if use_bwd:
                bwd = fb[:, sl] * wb_ref[...]
                val = bwd if val is None else val + bwd
            o_ref[0, n] = val.astype(o_ref.dtype)

    @pl.when(jnp.logical_and(half == 0, first_tile))
    def _():
        emit(True, True)

    @pl.when(jnp.logical_and(half == 0, jnp.logical_not(first_tile)))
    def _():
        emit(True, False)

    @pl.when(half == 1)
    def _():
        emit(False, True)


def _hy_kernels(tabs, w1p, b1, w2, b2, w3, b3, freq, w4f, w4b, seq, width):
    zz, wf, wb = tabs
    hid = w2.shape[0]
    tr = _tile(seq, FILT_ROWS)
    nt = seq // tr
    emb = zz.shape[1]
    full = lambda shape: pl.BlockSpec(shape, lambda hf, i: tuple(0 for _ in shape))
    return pl.pallas_call(
        _filt_kernel,
        grid=(2, nt),
        in_specs=[pl.BlockSpec((tr, emb), lambda hf, i: (hf * nt + i, 0)),
                  full((emb, hid)), full((1, hid)), full((hid, hid)), full((1, hid)),
                  full((hid, hid)), full((1, hid)), full((1, hid)),
                  full((hid, 2 * width)), full((hid, 2 * width)),
                  pl.BlockSpec((tr, width), lambda hf, i: (hf * nt + i, 0)),
                  pl.BlockSpec((tr, width), lambda hf, i: (hf * nt + i, 0))],
        out_specs=pl.BlockSpec((1, 2, tr, width), lambda hf, i: (hf, 0, i, 0)),
        out_shape=jax.ShapeDtypeStruct((2, 2, seq, width), BF16),
        compiler_params=_cparams("arbitrary", "arbitrary"),
        name="hy_filter",
    )(zz, w1p, b1.reshape(1, hid), w2, b2.reshape(1, hid), w3, b3.reshape(1, hid),
      freq.reshape(1, hid), w4f, w4b, wf, wb)


def _fft_a_kernel(m_ref, xa_ref, xb_ref, o_ref):
    d = jnp.concatenate([xa_ref[0].astype(BF16), xb_ref[0].astype(BF16)], axis=0)
    o_ref[0] = jnp.dot(m_ref[...], d, preferred_element_type=F32).astype(o_ref.dtype)


def _fft_a(m1, xa, xb, pairs, amap, bmap):
    nh, cols = xa.shape[1], xa.shape[2]
    rows = m1.shape[0]
    tc = _tile(cols, FFT_COLS)
    return pl.pallas_call(
        _fft_a_kernel,
        grid=(cols // tc, pairs),
        in_specs=[pl.BlockSpec(m1.shape, lambda j, p: (0, 0)),
                  pl.BlockSpec((1, nh, tc), lambda j, p: (amap(p), 0, j)),
                  pl.BlockSpec((1, nh, tc), lambda j, p: (bmap(p), 0, j))],
        out_specs=pl.BlockSpec((1, rows, tc), lambda j, p: (p, 0, j)),
        out_shape=jax.ShapeDtypeStruct((pairs, rows, cols), BF16),
        compiler_params=_cparams("arbitrary", "arbitrary"),
        name="fft_a",
    )(m1, xa, xb)


def _fft_b_kernel(are_ref, aim_ref, g_ref, gi_ref, kf_ref, v_ref):
    kb = are_ref.shape[1]
    half = are_ref.shape[2]
    for q in range(kb):
        d = jnp.concatenate([are_ref[0, q], aim_ref[0, q]], axis=0)
        x = jnp.dot(g_ref[q], d, preferred_element_type=F32)
        xr, xi = x[:half], x[half:]
        kr, ki = kf_ref[q, :half], kf_ref[q, half:]
        p = jnp.concatenate([xr * kr - xi * ki, xr * ki + xi * kr], axis=0).astype(BF16)
        v = jnp.dot(gi_ref[q], p, preferred_element_type=F32)
        v_ref[0, 0, q] = v[:half].astype(v_ref.dtype)
        v_ref[0, 1, q] = v[half:].astype(v_ref.dtype)


def _fft_b(a4, g, gi, kf, order, n2):
    pairs, _, half, width = a4.shape
    kb = _tile(n2, FFT_K2)
    nk = n2 // kb
    return pl.pallas_call(
        _fft_b_kernel,
        grid=(nk, pairs),
        in_specs=[pl.BlockSpec((1, kb, half, width), lambda i, p: (p, i, 0, 0)),
                  pl.BlockSpec((1, kb, half, width), lambda i, p: (p, nk + i, 0, 0)),
                  pl.BlockSpec((kb, 2 * half, 2 * half), lambda i, p: (i, 0, 0)),
                  pl.BlockSpec((kb, 2 * half, 2 * half), lambda i, p: (i, 0, 0)),
                  pl.BlockSpec((None, kb, 2 * half, width), lambda i, p: (order, i, 0, 0))],
        out_specs=pl.BlockSpec((1, 2, kb, half, width), lambda i, p: (p, 0, i, 0, 0)),
        out_shape=jax.ShapeDtypeStruct((pairs, 2, n2, half, width), BF16),
        compiler_params=_cparams("arbitrary", "arbitrary"),
        name="fft_b",
    )(a4, a4, g, gi, kf)


def _fft_bf_kernel(are_ref, aim_ref, g_ref, kf_ref):
    kb = are_ref.shape[1]
    for q in range(kb):
        d = jnp.concatenate([are_ref[0, q], aim_ref[0, q]], axis=0)
        kf_ref[0, q] = jnp.dot(g_ref[q], d, preferred_element_type=F32)


def _fft_bf(a4, g, n2):
    pairs, _, half, width = a4.shape
    kb = _tile(n2, FFT_K2)
    nk = n2 // kb
    return pl.pallas_call(
        _fft_bf_kernel,
        grid=(nk, pairs),
        in_specs=[pl.BlockSpec((1, kb, half, width), lambda i, p: (p, i, 0, 0)),
                  pl.BlockSpec((1, kb, half, width), lambda i, p: (p, nk + i, 0, 0)),
                  pl.BlockSpec((kb, 2 * half, 2 * half), lambda i, p: (i, 0, 0))],
        out_specs=pl.BlockSpec((1, kb, 2 * half, width), lambda i, p: (p, i, 0, 0)),
        out_shape=jax.ShapeDtypeStruct((pairs, n2, 2 * half, width), F32),
        compiler_params=_cparams("arbitrary", "arbitrary"),
        name="fft_bf",
    )(a4, a4, g)


def _fft_c_kernel(m_ref, v_ref, xg_ref, z_ref, db_ref, o_ref):
    y = jnp.dot(m_ref[...], v_ref[0], preferred_element_type=F32)
    nh = y.shape[0] // 2
    for s in range(2):
        conv = y[s * nh:(s + 1) * nh]
        z = z_ref[s].astype(F32)
        o_ref[s] = (xg_ref[s].astype(F32) * (conv + db_ref[...] * z)).astype(o_ref.dtype)


def _fft_c(m2, v, xg, xg_blk0, z, bsz, dbias_t):
    _, nh, cols = z.shape
    pairs, rows, _ = v.shape
    tc = _tile(cols, FFT_COLS)
    return pl.pallas_call(
        _fft_c_kernel,
        grid=(cols // tc, pairs),
        in_specs=[pl.BlockSpec(m2.shape, lambda j, p: (0, 0)),
                  pl.BlockSpec((1, rows, tc), lambda j, p: (p, 0, j)),
                  pl.BlockSpec((2, nh, tc), lambda j, p: (xg_blk0 + p, 0, j)),
                  pl.BlockSpec((2, nh, tc), lambda j, p: (p, 0, j)),
                  pl.BlockSpec((1, tc), lambda j, p: (0, j))],
        out_specs=pl.BlockSpec((2, nh, tc), lambda j, p: (p, 0, j)),
        out_shape=jax.ShapeDtypeStruct((bsz, nh, cols), BF16),
        compiler_params=_cparams("arbitrary", "arbitrary"),
        name="fft_c",
    )(m2, v, xg, z, dbias_t)


def _stack_complex(zr, zi):
    return jnp.concatenate([jnp.concatenate([zr, -zi], axis=-1),
                            jnp.concatenate([zi, zr], axis=-1)], axis=-2)


def _dft_tables(seq):
    n = 2 * seq
    n2 = n // LANES
    nh = n2 // 2
    ar = lambda m: jnp.arange(m, dtype=I32)

    def cis(num, den, sign):
        ang = (2.0 * math.pi / den) * (num % den).astype(F32)
        return jnp.cos(ang), sign * jnp.sin(ang)

    fr, fi = cis(ar(n2)[:, None] * ar(n2)[None, :], n2, -1.0)
    m1_data = _stack_complex(fr[:, :nh], fi[:, :nh])
    m1_filt = jnp.concatenate([fr, fi], axis=0)
    m2 = _stack_complex(fr.T[:nh] / n, -fi.T[:nh] / n)
    k2 = ar(n2)[:, None, None]
    k1 = ar(LANES)[None, :, None]
    n1 = ar(LANES)[None, None, :]
    er, ei = cis(n1 * (n2 * k1 + k2), n, -1.0)
    g = _stack_complex(er, ei)
    gi = _stack_complex(jnp.swapaxes(er, 1, 2), -jnp.swapaxes(ei, 1, 2))
    return (m1_data.astype(BF16), m1_filt.astype(BF16), m2.astype(BF16), g.astype(BF16), gi.astype(BF16), n2)


def _hy_tables(seq, width):
    pos = jnp.arange(seq, dtype=F32)
    t = jnp.linspace(0.0, 1.0, seq, dtype=F32)[:, None]
    bands = jnp.linspace(1e-4, HY_BANDS - 1, HY_BANDS, dtype=F32)
    ang = (2.0 * math.pi / seq) * pos[:, None] * bands[None, :]
    z = jnp.concatenate([t, jnp.cos(ang), -jnp.sin(ang)], axis=-1)
    deltas = jnp.abs(jnp.linspace(math.log(HY_TARGET) / HY_FAST_DECAY,
                                  math.log(HY_TARGET) / HY_SLOW_DECAY, width, dtype=F32))
    window = jnp.exp(-t * deltas[None, :]) + HY_SHIFT
    mirror = jnp.concatenate([jnp.zeros((1,), I32), jnp.arange(seq - 1, 0, -1, dtype=I32)])
    zz = jnp.concatenate([z, z[mirror]], axis=0)
    emb = z.shape[1]
    emb_pad = ((emb + 63) // 64) * 64
    zz = jnp.pad(zz, ((0, 0), (0, emb_pad - emb)))
    zero = jnp.zeros_like(window)
    first = (jnp.arange(seq) == 0)[:, None]
    wf = jnp.concatenate([window, zero], axis=0)
    wb = jnp.concatenate([jnp.where(first, window, 0.0),
                          jnp.where(first, 0.0, window[mirror])], axis=0)
    return zz, wf, wb


def _hyena(proj3, p, l, dft, tabs, col0, width):
    bsz, seq, _ = proj3.shape
    m1_data, m1_filt, m2, g, gi, n2 = dft
    nh = n2 // 2
    cols = LANES * width
    pairs = bsz // 2
    emb_pad = tabs[0].shape[1]
    w1p = jnp.pad(p['hf_w1'][l], ((0, emb_pad - p['hf_w1'].shape[1]), (0, 0)))
    hid = p['hf_w4'].shape[1]
    w4 = p['hf_w4'][l].reshape(hid, 2, 2, width)
    w4f = w4[:, :, 0, :].reshape(hid, 2 * width).astype(BF16)
    w4b = w4[:, :, 1, :].reshape(hid, 2 * width).astype(BF16)
    kern = _hy_kernels(tabs, w1p, p['hf_b1'][l], p['hf_w2'][l], p['hf_b2'][l], p['hf_w3'][l], p['hf_b3'][l],
                       p['hf_freq'][l], w4f, w4b, seq, width)
    kern = kern.reshape(4, nh, cols)
    ka = _fft_a(m1_filt, kern, kern, 2, lambda q: q, lambda q: 2 + q)
    kf = _fft_bf(ka.reshape(2, 2 * n2, LANES, width), g, n2)

    vxx = _hy_short(proj3, p['hy_conv_w'][l], p['hy_conv_b'][l], col0, width)
    vxx = vxx.reshape(3 * bsz, nh, cols)
    z = vxx
    for order in range(2):
        a = _fft_a(m1_data, z, z, pairs, lambda q: 2 * q, lambda q: 2 * q + 1)
        v = _fft_b(a.reshape(pairs, 2 * n2, LANES, width), g, gi, kf, order, n2)
        dbias_t = jnp.tile(p['hy_bias'][l, order], LANES).reshape(1, cols)
        z = _fft_c(m2, v.reshape(pairs, 2 * n2, cols), vxx, (1 + order) * pairs, z, bsz, dbias_t)
    return z.reshape(bsz, seq, width)


CF_ROWS = 64


def _conf_kernel(ap_ref, a_ref, an_ref, gp_ref, g_ref, gn_ref, w_ref, b_ref, lg_ref, lb_ref,
                 o_ref, u_scr, acc_scr):
    i = pl.program_id(1)
    last = pl.num_programs(1) - 1
    tl = a_ref.shape[1]
    width = a_ref.shape[2]

    def glu(a, g):
        return a.astype(F32) * jax.nn.sigmoid(g.astype(F32))

    u_scr[0:CF_HALO] = jnp.where(i > 0, glu(ap_ref[0], gp_ref[0]), 0.0)
    u_scr[CF_HALO:CF_HALO + tl] = glu(a_ref[0], g_ref[0])
    u_scr[CF_HALO + tl:2 * CF_HALO + tl] = jnp.where(i < last, glu(an_ref[0], gn_ref[0]), 0.0)

    shift = CF_HALO - CF_KERNEL // 2

    def rows(r, carry):
        r0 = pl.multiple_of(r * CF_ROWS, CF_ROWS)
        for c in range(width // LANES):
            cs = slice(c * LANES, (c + 1) * LANES)
            win = u_scr[pl.ds(r0, CF_ROWS + 2 * CF_HALO), cs]
            n_win = win.shape[0]
            rolled = [win] + [pltpu.roll(win, n_win - b, 0) for b in range(1, SUBLANES)]
            acc = jnp.zeros((CF_ROWS, LANES), F32)
            for j in range(CF_KERNEL):
                a, b = divmod(shift + j, SUBLANES)
                acc = acc + w_ref[j:j + 1, cs] * rolled[b][a * SUBLANES:a * SUBLANES + CF_ROWS]
            acc_scr[pl.ds(r0, CF_ROWS), cs] = acc
        return carry

    lax.fori_loop(0, tl // CF_ROWS, rows, 0)
    y = acc_scr[...] + b_ref[...]
    mu = jnp.mean(y, axis=-1, keepdims=True)
    yc = y - mu
    var = jnp.mean(yc * yc, axis=-1, keepdims=True)
    yn = yc * lax.rsqrt(var + EPS) * lg_ref[...] + lb_ref[...]
    o_ref[0] = (yn * jax.nn.sigmoid(yn)).astype(o_ref.dtype)


def _conformer(proj3, dw_w, dw_b, ln_g, ln_b, col0, width):
    bsz, seq, _ = proj3.shape
    tl = _tile(seq, CONF_ROWS_TILE)
    a_blk = col0 // width
    g_blk = a_blk + 1
    hpt = tl // CF_HALO
    nhalo = seq // CF_HALO

    def cur(blk):
        return pl.BlockSpec((1, tl, width), lambda b, i: (b, i, blk))

    def prev(blk):
        return pl.BlockSpec((1, CF_HALO, width), lambda b, i: (b, jnp.maximum(i * hpt - 1, 0), blk))

    def nxt(blk):
        return pl.BlockSpec((1, CF_HALO, width), lambda b, i: (b, jnp.minimum((i + 1) * hpt, nhalo - 1), blk))

    vec = lambda rows: pl.BlockSpec((rows, width), lambda b, i: (0, 0))
    return pl.pallas_call(
        _conf_kernel,
        grid=(bsz, seq // tl),
        in_specs=[prev(a_blk), cur(a_blk), nxt(a_blk), prev(g_blk), cur(g_blk), nxt(g_blk),
                  vec(CF_KERNEL), vec(1), vec(1), vec(1)],
        out_specs=pl.BlockSpec((1, tl, width), lambda b, i: (b, i, 0)),
        out_shape=jax.ShapeDtypeStruct((bsz, seq, width), BF16),
        scratch_shapes=[pltpu.VMEM((tl + 2 * CF_HALO, width), F32), pltpu.VMEM((tl, width), F32)],
        compiler_params=_cparams("arbitrary", "arbitrary"),
        name="conformer",
    )(proj3, proj3, proj3, proj3, proj3, proj3, dw_w, dw_b.reshape(1, width),
      ln_g.reshape(1, width), ln_b.reshape(1, width))


def _merge_kernel(oa_ref, oh_ref, oc_ref, ga_ref, gh_ref, gc_ref, wb_ref, wo_ref, x_ref, g1_ref,
                  n2_ref, sh2_ref, sc2_ref, wr_ref, br_ref, xo_ref, h2_ref, lg_ref):
    merged = None
    for n, (o_ref, gt_ref) in enumerate(((oa_ref, ga_ref), (oh_ref, gh_ref), (oc_ref, gc_ref))):
        br = jnp.dot(o_ref[...], wb_ref[n], preferred_element_type=F32)
        term = br * jax.nn.sigmoid(gt_ref[...].astype(F32))
        merged = term if merged is None else merged + term
    y = jnp.dot(merged.astype(BF16), wo_ref[...], preferred_element_type=F32)
    xn = x_ref[...] + g1_ref[0] * y
    xo_ref[...] = xn
    hn = xn * lax.rsqrt(jnp.mean(xn * xn, axis=-1, keepdims=True) + EPS)
    h2 = (hn * n2_ref[...]) * (1.0 + sc2_ref[0]) + sh2_ref[0]
    h_hi = h2.astype(BF16)
    h2_ref[...] = h_hi
    h_lo = (h2 - h_hi.astype(F32)).astype(BF16)
    lg = jnp.dot(h_hi, wr_ref[0], preferred_element_type=F32)
    lg = lg + jnp.dot(h_lo, wr_ref[0], preferred_element_type=F32)
    lg = lg + jnp.dot(h_hi, wr_ref[1], preferred_element_type=F32)
    lg_ref[...] = lg + br_ref[...]


def _merge(o_att, o_hy, o_cf, proj2, gate_col0, wb_bf, wo_bf, x2, mod3, norm2_g, wr_pad, br_pad, seq):
    t, d = x2.shape
    bw = o_att.shape[1]
    tm = _tile(seq, MERGE_ROWS)
    per_b = seq // tm
    gb = gate_col0 // d
    epad = wr_pad.shape[2]
    row = lambda w: pl.BlockSpec((tm, w), lambda i: (i, 0))
    modc = lambda c: pl.BlockSpec((1, 1, d), lambda i: (i // per_b, 0, c))
    return pl.pallas_call(
        _merge_kernel,
        grid=(t // tm,),
        in_specs=[row(bw), row(bw), row(bw),
                  pl.BlockSpec((tm, d), lambda i: (i, gb)),
                  pl.BlockSpec((tm, d), lambda i: (i, gb + 1)),
                  pl.BlockSpec((tm, d), lambda i: (i, gb + 2)),
                  pl.BlockSpec((3, bw, d), lambda i: (0, 0, 0)),
                  pl.BlockSpec((d, d), lambda i: (0, 0)),
                  row(d),
                  modc(2),
                  pl.BlockSpec((1, d), lambda i: (0, 0)),
                  modc(3), modc(4),
                  pl.BlockSpec((2, d, epad), lambda i: (0, 0, 0)),
                  pl.BlockSpec((1, epad), lambda i: (0, 0))],
        out_specs=[row(d), row(d), row(epad)],
        out_shape=[jax.ShapeDtypeStruct((t, d), F32), jax.ShapeDtypeStruct((t, d), BF16),
                   jax.ShapeDtypeStruct((t, epad), F32)],
        compiler_params=_cparams("arbitrary"),
        name="merge",
    )(o_att, o_hy, o_cf, proj2, proj2, proj2, wb_bf, wo_bf, x2, mod3, norm2_g.reshape(1, d),
      mod3, mod3, wr_pad, br_pad)


SEG_ALIGN = 8
SEG_CHUNK = 64


def _router_kernel(lg_ref, tri_ref, low_ref, qw_ref, tab_ref, cnt_ref, carry_scr, *, n_exp):
    @pl.when(pl.program_id(0) == 0)
    def _():
        carry_scr[...] = jnp.zeros_like(carry_scr)

    lt = lg_ref[...].T[:n_exp]
    eio = lax.broadcasted_iota(I32, lt.shape, 0)
    cur = lt
    vals, idxs = [], []
    for _ in range(TOP_K):
        m = jnp.max(cur, axis=0, keepdims=True)
        ik = jnp.min(jnp.where(cur == m, eio, n_exp), axis=0, keepdims=True)
        vals.append(m)
        idxs.append(ik)
        cur = jnp.where(eio == ik, -jnp.inf, cur)
    ex = [jnp.exp(v - vals[0]) for v in vals]
    den = ex[0] + ex[1] + ex[2] + ex[3]
    onehot = jnp.zeros(lt.shape, F32)
    for ik in idxs:
        onehot = onehot + jnp.where(eio == ik, 1.0, 0.0)
    incl = jnp.dot(onehot.astype(BF16), tri_ref[...], preferred_element_type=F32)
    count = incl[:, incl.shape[1] - 1:]
    padded = jnp.floor((count + (SEG_ALIGN - 1)) * (1.0 / SEG_ALIGN)) * SEG_ALIGN
    padded_b = jnp.broadcast_to(padded, carry_scr.shape)
    seg_start = jnp.dot(low_ref[...], padded_b, precision=lax.Precision.HIGHEST,
                        preferred_element_type=F32)
    local = incl - onehot + seg_start[:, 0:1]
    for k in range(TOP_K):
        qw_ref[k:k + 1, :] = jnp.sum(jnp.where(eio == idxs[k], local, 0.0), axis=0, keepdims=True)
        qw_ref[TOP_K + k:TOP_K + k + 1, :] = ex[k] / den
    carry = carry_scr[...]
    tab_ref[0, 0] = seg_start
    tab_ref[0, 1] = carry
    tab_ref[0, 2] = padded_b
    carry_scr[...] = carry + padded_b
    cnt_ref[...] = carry + padded_b


def _router(logits_pad, n_exp, tt):
    t, epad = logits_pad.shape
    n_tt = t // tt
    tri = (jnp.arange(tt)[:, None] <= jnp.arange(tt)[None, :]).astype(BF16)
    low = (jnp.arange(n_exp)[None, :] < jnp.arange(n_exp)[:, None]).astype(F32)
    return pl.pallas_call(
        functools.partial(_router_kernel, n_exp=n_exp),
        grid=(n_tt,),
        in_specs=[pl.BlockSpec((tt, epad), lambda i: (i, 0)),
                  pl.BlockSpec((tt, tt), lambda i: (0, 0)),
                  pl.BlockSpec((n_exp, n_exp), lambda i: (0, 0))],
        out_specs=[pl.BlockSpec((2 * TOP_K, tt), lambda i: (0, i)),
                   pl.BlockSpec((1, 3, n_exp, LANES), lambda i: (i, 0, 0, 0)),
                   pl.BlockSpec((n_exp, LANES), lambda i: (0, 0))],
        out_shape=[jax.ShapeDtypeStruct((2 * TOP_K, t), F32),
                   jax.ShapeDtypeStruct((n_tt, 3, n_exp, LANES), F32),
                   jax.ShapeDtypeStruct((n_exp, LANES), F32)],
        scratch_shapes=[pltpu.VMEM((n_exp, LANES), F32)],
        compiler_params=_cparams("arbitrary"),
        name="router",
    )(logits_pad, tri, low)


def _row_window(ref, base, off, size):
    return ref.at[pl.ds(pl.multiple_of(base + off, SEG_ALIGN), size)]


def _seg_copies(src, src_base, dst, dst_base, length, sem, act):
    n_full = length // SEG_CHUNK

    def full(c, carry):
        off = c * SEG_CHUNK
        act(pltpu.make_async_copy(_row_window(src, src_base, off, SEG_CHUNK),
                                  _row_window(dst, dst_base, off, SEG_CHUNK), sem))
        return carry

    lax.fori_loop(0, n_full, full, 0)
    off = n_full * SEG_CHUNK
    size = SEG_CHUNK // 2
    while size >= SEG_ALIGN:
        bit = (length & size) != 0

        @pl.when(bit)
        def _(off=off, size=size):
            act(pltpu.make_async_copy(_row_window(src, src_base, off, size),
                                      _row_window(dst, dst_base, off, size), sem))

        off = off + jnp.where(bit, size, 0)
        size //= 2


def _start(copy):
    copy.start()


def _wait(copy):
    copy.wait()


def _sort_chunks(rows_total):
    n = 3 if rows_total % (3 * LANES) == 0 else 1
    return n, rows_total // n


def _dispatch_kernel(tab_s, tab_d, tab_m, tail_d, tail_z, n_valid, h_ref, q_ref, xs_hbm, sbuf, zbuf, sem, zsem,
                     *, n_exp):
    i = pl.program_id(0)
    n = pl.num_programs(0)
    slot = i % 2
    tt = h_ref.shape[0]
    rows_total = sbuf.shape[1]
    zrows = zbuf.shape[0]

    def tile_copies(j, s, act):
        def per_expert(e, carry):
            k = j * n_exp + e
            _seg_copies(sbuf.at[s], tab_s[k], xs_hbm, tab_d[k], tab_m[k], sem.at[s], act)
            return carry
        lax.fori_loop(0, n_exp, per_expert, 0)

    def tail_copies(act):
        def per_expert(e, carry):
            off = tail_d[e]
            size = zbuf.shape[0]
            while size >= SEG_ALIGN:
                bit = (tail_z[e] & size) != 0

                @pl.when(bit)
                def _(off=off, size=size):
                    act(pltpu.make_async_copy(zbuf.at[pl.ds(0, size)], _row_window(xs_hbm, off, 0, size),
                                              zsem.at[0]))

                off = off + jnp.where(bit, size, 0)
                size //= 2
            return carry
        lax.fori_loop(0, n_exp, per_expert, 0)

        def per_unused_tile(tile, carry):
            for half in range(2):
                act(pltpu.make_async_copy(zbuf, _row_window(xs_hbm, tile * (2 * zrows), half * zrows, zrows),
                                          zsem.at[0]))
            return carry
        lax.fori_loop(n_valid[0], xs_hbm.shape[0] // (2 * zrows), per_unused_tile, 0)

    @pl.when(i == 0)
    def _():
        zbuf[...] = jnp.zeros_like(zbuf)
        tail_copies(_start)

    @pl.when(i >= 2)
    def _():
        tile_copies(i - 2, slot, _wait)

    hb = h_ref[...].astype(BF16)
    n_chunks, rc = _sort_chunks(rows_total)
    for c in range(n_chunks):
        row = lax.broadcasted_iota(I32, (rc, tt), 0) + c * rc
        pi = jnp.zeros((rc, tt), F32)
        for k in range(TOP_K):
            pi = jnp.where(row == q_ref[k:k + 1, :].astype(I32), 1.0, pi)
        sbuf[slot, c * rc:(c + 1) * rc] = jnp.dot(pi.astype(BF16), hb, preferred_element_type=F32)
    tile_copies(i, slot, _start)

    @pl.when(i == n - 1)
    def _():
        @pl.when(i >= 1)
        def _():
            tile_copies(i - 1, 1 - slot, _wait)
        tile_copies(i, slot, _wait)
        tail_copies(_wait)


def _dispatch(h2, q, tabs, tails, n_valid, n_rows, tt, n_exp, zrows):
    t, d = h2.shape
    rows_total = TOP_K * tt + SEG_ALIGN * n_exp
    grid_spec = pltpu.PrefetchScalarGridSpec(
        num_scalar_prefetch=6,
        grid=(t // tt,),
        in_specs=[pl.BlockSpec((tt, d), lambda i, *_: (i, 0)),
                  pl.BlockSpec((2 * TOP_K, tt), lambda i, *_: (0, i))],
        out_specs=pl.BlockSpec(memory_space=pl.ANY),
        scratch_shapes=[pltpu.VMEM((2, rows_total, d), F32), pltpu.VMEM((zrows, d), F32),
                        pltpu.SemaphoreType.DMA((2,)), pltpu.SemaphoreType.DMA((1,))],
    )
    return pl.pallas_call(
        functools.partial(_dispatch_kernel, n_exp=n_exp),
        grid_spec=grid_spec,
        out_shape=jax.ShapeDtypeStruct((n_rows, d), F32),
        compiler_params=_cparams("arbitrary"),
        name="dispatch",
    )(*tabs, *tails, n_valid, h2, q)


def _expert_kernel(te_ref, nv_ref, x_ref, wgu_ref, bgu_ref, wdn_ref, bdn_ref, o_ref, wgu_bf, wdn_bf):
    i = pl.program_id(0)
    ff = wdn_ref.shape[2]
    valid = i < nv_ref[0]
    new_expert = jnp.logical_or(i == 0, te_ref[i] != te_ref[jnp.maximum(i - 1, 0)])

    @pl.when(jnp.logical_and(valid, new_expert))
    def _():
        wgu_bf[...] = wgu_ref[0, 0].astype(BF16)
        wdn_bf[...] = wdn_ref[0, 0].astype(BF16)

    @pl.when(valid)
    def _():
        xb = x_ref[...].astype(BF16)
        gu = jnp.dot(xb, wgu_bf[...], preferred_element_type=F32) + bgu_ref[0, 0]
        g = jnp.minimum(gu[:, :ff], SWIGLU_LIMIT)
        u = jnp.clip(gu[:, ff:], -SWIGLU_LIMIT, SWIGLU_LIMIT)
        hd = (u + 1.0) * (g * jax.nn.sigmoid(SWIGLU_ALPHA * g))
        y = jnp.dot(hd.astype(BF16), wdn_bf[...], preferred_element_type=F32) + bdn_ref[0, 0]
        o_ref[...] = y.astype(o_ref.dtype)

    @pl.when(jnp.logical_not(valid))
    def _():
        o_ref[...] = jnp.zeros_like(o_ref)


def _experts(xs, tile_e, n_valid, w_gu, b_gu, w_down, b_down, layer, tm):
    n_rows, d = xs.shape
    n_tiles = n_rows // tm
    depth, n_exp, _, ff2 = w_gu.shape
    ff = ff2 // 2
    wsel = lambda i, te, nv: (layer, te[i], 0, 0)
    grid_spec = pltpu.PrefetchScalarGridSpec(
        num_scalar_prefetch=2,
        grid=(n_tiles,),
        in_specs=[pl.BlockSpec((tm, d), lambda i, te, nv: (jnp.minimum(i, nv[0] - 1), 0)),
                  pl.BlockSpec((1, 1, d, ff2), wsel),
                  pl.BlockSpec((1, 1, 1, ff2), wsel),
                  pl.BlockSpec((1, 1, ff, d), wsel),
                  pl.BlockSpec((1, 1, 1, d), wsel)],
        out_specs=pl.BlockSpec((tm, d), lambda i, te, nv: (i, 0)),
        scratch_shapes=[pltpu.VMEM((d, ff2), BF16), pltpu.VMEM((ff, d), BF16)],
    )
    return pl.pallas_call(
        _expert_kernel,
        grid_spec=grid_spec,
        out_shape=jax.ShapeDtypeStruct((n_rows, d), F32),
        compiler_params=_cparams("arbitrary"),
        name="experts",
    )(tile_e, n_valid, xs, w_gu, b_gu.reshape(depth, n_exp, 1, ff2), w_down, b_down.reshape(depth, n_exp, 1, d))


def _combine_kernel(tab_s, tab_d, tab_m, ys_hbm, qw_ref, x_ref, g2_ref, o_ref, cbuf, sem, *, n_exp):
    i = pl.program_id(0)
    n = pl.num_programs(0)
    slot = i % 2
    tt = x_ref.shape[0]
    rows_total = cbuf.shape[1]

    def tile_copies(j, s, act):
        def per_expert(e, carry):
            k = j * n_exp + e
            _seg_copies(ys_hbm, tab_d[k], cbuf.at[s], tab_s[k], tab_m[k], sem.at[s], act)
            return carry
        lax.fori_loop(0, n_exp, per_expert, 0)

    @pl.when(i == 0)
    def _():
        cbuf[...] = jnp.zeros_like(cbuf)
        tile_copies(0, 0, _start)

    @pl.when(i + 1 < n)
    def _():
        tile_copies(i + 1, 1 - slot, _start)

    tile_copies(i, slot, _wait)
    qw = qw_ref[...]
    qw_t = jnp.concatenate([qw, jnp.zeros((LANES - qw.shape[0], tt), F32)], axis=0).T
    q = qw_t[:, :TOP_K].astype(I32)
    w = qw_t[:, TOP_K:2 * TOP_K]
    n_chunks, rc = _sort_chunks(rows_total)
    acc = jnp.zeros(x_ref.shape, F32)
    for c in range(n_chunks):
        col = lax.broadcasted_iota(I32, (tt, rc), 1) + c * rc
        pw = jnp.zeros((tt, rc), F32)
        for k in range(TOP_K):
            pw = jnp.where(col == q[:, k:k + 1], w[:, k:k + 1], pw)
        acc = acc + jnp.dot(pw.astype(BF16), cbuf[slot, c * rc:(c + 1) * rc].astype(BF16),
                            preferred_element_type=F32)
    o_ref[...] = x_ref[...] + g2_ref[0] * acc


def _combine(ys, qw, tabs, x2, mod3, seq, tt, n_exp):
    t, d = x2.shape
    per_b = seq // tt
    rows_total = TOP_K * tt + SEG_ALIGN * n_exp
    grid_spec = pltpu.PrefetchScalarGridSpec(
        num_scalar_prefetch=3,
        grid=(t // tt,),
        in_specs=[pl.BlockSpec(memory_space=pl.ANY),
                  pl.BlockSpec((2 * TOP_K, tt), lambda i, *_: (0, i)),
                  pl.BlockSpec((tt, d), lambda i, *_: (i, 0)),
                  pl.BlockSpec((1, 1, d), lambda i, *_: (i // per_b, 0, 5))],
        out_specs=pl.BlockSpec((tt, d), lambda i, *_: (i, 0)),
        scratch_shapes=[pltpu.VMEM((2, rows_total, d), F32), pltpu.SemaphoreType.DMA((2,))],
    )
    return pl.pallas_call(
        functools.partial(_combine_kernel, n_exp=n_exp),
        grid_spec=grid_spec,
        out_shape=jax.ShapeDtypeStruct((t, d), F32),
        compiler_params=_cparams("arbitrary"),
        name="combine",
    )(*tabs, ys, qw, x2, mod3)


def _moe(h2, logits_pad, x2, mod3, w_gu, b_gu, w_down, b_down, layer, seq):
    t, d = x2.shape
    n_exp = w_gu.shape[1]
    tt = _tile(seq, MOE_TOKENS)
    tm = _tile(t * TOP_K // n_exp, MOE_ROWS)
    n_tt = t // tt
    qw, tab, cnt = _router(logits_pad, n_exp, tt)
    tab = tab[:, :, :, 0].astype(I32)
    used = cnt[:, 0].astype(I32)
    group = ((used + tm - 1) // tm) * tm
    gend = jnp.cumsum(group)
    goff = gend - group
    max_rows = TOP_K * t + n_tt * n_exp * (SEG_ALIGN - 1)
    n_tiles = -(-max_rows // tm) + n_exp
    tile_start = jnp.arange(n_tiles, dtype=I32) * tm
    tile_e = jnp.minimum(jnp.sum((tile_start[:, None] >= gend[None, :]).astype(I32), axis=1), n_exp - 1)
    n_valid = (gend[-1:] // tm).astype(I32)
    tabs = (tab[:, 0].reshape(-1), (tab[:, 1] + goff[None, :]).reshape(-1), tab[:, 2].reshape(-1))
    tails = (goff + used, group - used)
    xs = _dispatch(h2, qw, tabs, tails, n_valid, n_tiles * tm, tt, n_exp, tm // 2)
    ys = _experts(xs, tile_e, n_valid, w_gu, b_gu, w_down, b_down, layer, tm)
    return _combine(ys, qw, tabs, x2, mod3, seq, tt, n_exp)


def kernel(x, c, norm1_g, norm2_g, w_ada, b_ada, w_in, b_in, q_norm_g, k_norm_g, lam_q1, lam_k1, lam_q2, lam_k2, subln_g, hy_conv_w, hy_conv_b, hf_w1, hf_b1, hf_w2, hf_b2, hf_w3, hf_b3, hf_freq, hf_w4, hy_bias, cf_dw_w, cf_dw_b, cf_ln_g, cf_ln_b, w_branch, w_out, w_router, b_router, w_gu, b_gu, w_down, b_down):
    bsz, seq, d = x.shape
    depth = w_ada.shape[0]
    t = bsz * seq
    bw = d // 2
    att_w = ATT_HEADS * 2 * HEAD_DIM
    hy_col0 = 3 * att_w
    cf_col0 = hy_col0 + 3 * bw
    gate_col0 = cf_col0 + 2 * bw
    n_exp = w_router.shape[2]
    assert att_w == bw and bsz % 2 == 0 and seq % LANES == 0

    hp = dict(hf_w1=hf_w1, hf_b1=hf_b1, hf_w2=hf_w2, hf_b2=hf_b2, hf_w3=hf_w3, hf_b3=hf_b3,
              hf_freq=hf_freq, hf_w4=hf_w4, hy_conv_w=hy_conv_w, hy_conv_b=hy_conv_b, hy_bias=hy_bias)

    pos = jnp.arange(seq, dtype=F32)
    inv = 1.0 / (ROPE_THETA ** (jnp.arange(0, HEAD_DIM, 2, dtype=F32) / HEAD_DIM))
    ang = pos[:, None] * inv[None, :]
    ang = jnp.concatenate([ang, ang], axis=-1)
    sign = jnp.where(jnp.arange(HEAD_DIM) < HEAD_DIM // 2, -1.0, 1.0).astype(F32)
    reps = att_w // HEAD_DIM
    cos_t = jnp.tile(jnp.cos(ang), (1, reps))
    sin_t = jnp.tile(jnp.sin(ang) * sign[None, :], (1, reps))
    lane = jnp.arange(att_w)
    gsum = (lane[:, None] // HEAD_DIM == lane[None, :] // HEAD_DIM).astype(BF16)
    dft = _dft_tables(seq)
    tabs = _hy_tables(seq, bw)
    epad = LANES

    mod_all = _ada_all(c, w_ada, b_ada)
    x2 = x.reshape(t, d)
    for l in range(depth):
        mod3 = mod_all[l].reshape(bsz, 1, 6 * d)
        lam_init = 0.8 - 0.6 * math.exp(-0.3 * l)

        proj2 = _in_proj(x2, mod3, norm1_g[l], w_in[l].astype(BF16), b_in[l], seq)
        proj3 = proj2.reshape(bsz, seq, -1)

        qr, kt = _qk_prep(proj3, cos_t, sin_t, jnp.tile(q_norm_g[l], reps).reshape(1, att_w),
                          jnp.tile(k_norm_g[l], reps).reshape(1, att_w), gsum)
        lamv = jnp.stack([lam_q1[l], lam_k1[l], lam_q2[l], lam_k2[l]])
        o_att = _attention(qr, kt, proj3, lamv, subln_g[l], lam_init)

        o_hy = _hyena(proj3, hp, l, dft, tabs, hy_col0, bw)
        o_cf = _conformer(proj3, cf_dw_w[l], cf_dw_b[l], cf_ln_g[l], cf_ln_b[l], cf_col0, bw)

        wr_pad = jnp.pad(w_router[l], ((0, 0), (0, epad - n_exp)))
        wr_hi = wr_pad.astype(BF16)
        wr_pad = jnp.stack([wr_hi, (wr_pad - wr_hi.astype(F32)).astype(BF16)])
        br_pad = jnp.pad(b_router[l], (0, epad - n_exp)).reshape(1, epad)
        x2, h2, logits = _merge(o_att.reshape(t, bw), o_hy.reshape(t, bw), o_cf.reshape(t, bw), proj2,
                                gate_col0, w_branch[l].astype(BF16), w_out[l].astype(BF16), x2, mod3,
                                norm2_g[l], wr_pad, br_pad, seq)
        x2 = _moe(h2, logits, x2, mod3, w_gu, b_gu, w_down, b_down, l, seq)
    return x2.reshape(bsz, seq, d)
```
